```python
import jax, jax.numpy as jnp
from jax import lax
import numpy as np

D_MODEL = 1024
BATCH = 8
SEQ = 2048
DEPTH = 4
DEC_BATCH = 128
DEC_SEQ = 8
PAST_LEN = 16384
PAGE_SIZE = 128

MIX_WIDTH = D_MODEL
WIDTH_A = MIX_WIDTH // 2
WIDTH_B = MIX_WIDTH - WIDTH_A
CHUNK = 128
HEADS_A = 8
HEAD_DIM_A = WIDTH_A // HEADS_A
POOL_WINDOWS = (2, 4, 8, 16)
POOL_GROUPS = len(POOL_WINDOWS)
POOL_GROUP_DIM = WIDTH_B // POOL_GROUPS
POOL_HIST = max(POOL_WINDOWS) - 1
PROJ_WIDTH = 2 * WIDTH_A + WIDTH_B
N_MEM = 256
X_HEADS = 4
X_HEAD_DIM = D_MODEL // X_HEADS
N_EXPERT_GROUPS = 4
EXPERTS_PER_GROUP = 4
N_EXPERTS = N_EXPERT_GROUPS * EXPERTS_PER_GROUP
TOP_K_INNER = 2
D_EXPERT = D_MODEL // 4
EPS = 1e-6

kernel_name = "hymba_gmlp_pool_hmoe_step"


def rmsnorm(x, g):
    xf = x.astype(jnp.float32)
    r = lax.rsqrt(jnp.mean(xf * xf, axis=-1, keepdims=True) + EPS)
    return (xf * r * g.astype(jnp.float32)).astype(x.dtype)


def chunk_spatial_gate(u, v, w_s, b_s):
    b, l, h, dh = v.shape
    n_chunks = -(-l // CHUNK)
    pad = n_chunks * CHUNK - l
    vp = jnp.pad(v, ((0, 0), (0, pad), (0, 0), (0, 0))).reshape(b, n_chunks, CHUNK, h, dh)
    causal = jnp.tril(jnp.ones((CHUNK, CHUNK), dtype=bool))
    ws = jnp.where(causal[None], w_s, jnp.zeros((), w_s.dtype))
    mixed = jnp.einsum('hij,bcjhd->bcihd', ws, vp) + b_s.T[None, None, :, :, None]
    mixed = mixed.reshape(b, n_chunks * CHUNK, h, dh)[:, :l]
    return u * mixed


def multiscale_pool(z_ext, pos0, w_pool, s_pool):
    b, le, _ = z_ext.shape
    l = le - POOL_HIST
    zf = z_ext.astype(jnp.float32)
    cs = jnp.concatenate([jnp.zeros((b, 1, WIDTH_B), jnp.float32), jnp.cumsum(zf, axis=1)], axis=1)
    end = cs[:, POOL_HIST + 1:]
    z_new = zf[:, POOL_HIST:]
    pos = pos0 + jnp.arange(l)
    outs = []
    for g, w in enumerate(POOL_WINDOWS):
        sl = slice(g * POOL_GROUP_DIM, (g + 1) * POOL_GROUP_DIM)
        start = cs[:, POOL_HIST + 1 - w:POOL_HIST + 1 - w + l, sl]
        cnt = jnp.minimum(w, pos + 1).astype(jnp.float32)[None, :, None]
        outs.append((end[..., sl] - start) / cnt - z_new[..., sl])
    pooled = jnp.stack(outs, axis=2).astype(z_ext.dtype)
    y = jnp.einsum('blgc,gcd->blgd', pooled, w_pool).reshape(b, l, WIDTH_B)
    return y * s_pool


def memory_attention(h, mem_k, mem_v, w_q, w_o):
    b, l, _ = h.shape
    q = (h @ w_q).reshape(b, l, X_HEADS, X_HEAD_DIM)
    s = jnp.einsum('blhd,bmhd->bhlm', q, mem_k).astype(jnp.float32) * (X_HEAD_DIM ** -0.5)
    p = jax.nn.softmax(s, axis=-1).astype(h.dtype)
    o = jnp.einsum('bhlm,bmhd->blhd', p, mem_v).reshape(b, l, D_MODEL)
    return o @ w_o


def hierarchical_moe(h, w_group, b_group, w_router, b_router, w_gate, w_up, w_down):
    b, l, d = h.shape
    t = h.reshape(b * l, d)
    g_logits = (t @ w_group).astype(jnp.float32) + b_group.astype(jnp.float32)
    g_prob = jax.nn.softmax(g_logits, axis=-1)
    g_idx = jnp.argmax(g_logits, axis=-1)
    g_p = jnp.max(g_prob, axis=-1, keepdims=True)
    e_logits = ((t @ w_router).astype(jnp.float32) + b_router.astype(jnp.float32)).reshape(
        -1, N_EXPERT_GROUPS, EXPERTS_PER_GROUP)
    e_sel = jnp.einsum('tge,tg->te', e_logits, jax.nn.one_hot(g_idx, N_EXPERT_GROUPS, dtype=jnp.float32))
    top_v, top_i = lax.top_k(e_sel, TOP_K_INNER)
    top_w = jax.nn.softmax(top_v, axis=-1) * g_p
    expert_id = g_idx[:, None] * EXPERTS_PER_GROUP + top_i
    gates = jnp.sum(jax.nn.one_hot(expert_id, N_EXPERTS, dtype=jnp.float32) * top_w[..., None], axis=1)
    hg = jnp.einsum('td,edf->tef', t, w_gate)
    hu = jnp.einsum('td,edf->tef', t, w_up)
    a = jax.nn.silu(hg) * hu * gates[..., None].astype(t.dtype)
    y = jnp.einsum('tef,efd->td', a, w_down)
    return y.reshape(b, l, d)


def layer_fwd(x, pos0, pool_hist, mem_k, mem_v,
              g_mix, w_in, g_v, w_s, b_s, w_pool, s_pool, w_out,
              g_xattn, w_xq, w_xo,
              g_ffn, w_group, b_group, w_router, b_router, w_gate, w_up, w_down):
    b, l, _ = x.shape
    h = rmsnorm(x, g_mix)
    proj = h @ w_in
    ua = jax.nn.gelu(proj[..., :2 * WIDTH_A])
    u = ua[..., :WIDTH_A].reshape(b, l, HEADS_A, HEAD_DIM_A)
    v = rmsnorm(ua[..., WIDTH_A:], g_v).reshape(b, l, HEADS_A, HEAD_DIM_A)
    z = proj[..., 2 * WIDTH_A:]
    a_out = chunk_spatial_gate(u, v, w_s, b_s).reshape(b, l, WIDTH_A)
    z_ext = jnp.concatenate([pool_hist.astype(z.dtype), z], axis=1)
    b_out = multiscale_pool(z_ext, pos0, w_pool, s_pool)
    x = x + jnp.concatenate([a_out, b_out], axis=-1) @ w_out
    x = x + memory_attention(rmsnorm(x, g_xattn), mem_k, mem_v, w_xq, w_xo)
    x = x + hierarchical_moe(rmsnorm(x, g_ffn), w_group, b_group, w_router, b_router, w_gate, w_up, w_down)
    last_chunk_start = ((l - 1) // CHUNK) * CHUNK
    return x, v[:, last_chunk_start:], z_ext[:, -POOL_HIST:]


def setup_inputs(seed: int = 0) -> dict:
    key = jax.random.key(seed)
    ks = jax.random.split(key, 32)
    f32 = jnp.float32

    def nrm(k, shape, s):
        return jax.random.normal(k, shape, f32) * s

    def gain(k, shape):
        return 1.0 + 0.02 * jax.random.normal(k, shape, f32)

    return {
        "x_prompt": nrm(ks[0], (BATCH, SEQ, D_MODEL), 1.0),
        "x_sample": nrm(ks[1], (DEC_BATCH, DEC_SEQ, D_MODEL), 1.0),
        "cache_mem_k": nrm(ks[2], (DEPTH, DEC_BATCH, N_MEM, X_HEADS, X_HEAD_DIM), 1.0),
        "cache_mem_v": nrm(ks[3], (DEPTH, DEC_BATCH, N_MEM, X_HEADS, X_HEAD_DIM), 1.0),
        "state_pool": nrm(ks[4], (DEPTH, DEC_BATCH, POOL_HIST, WIDTH_B), 1.0),
        "mem_prompt": nrm(ks[5], (BATCH, N_MEM, D_MODEL), 1.0),
        "g_mix": gain(ks[6], (DEPTH, D_MODEL)),
        "w_in": nrm(ks[7], (DEPTH, D_MODEL, PROJ_WIDTH), D_MODEL ** -0.5),
        "g_v": gain(ks[8], (DEPTH, WIDTH_A)),
        "w_s": nrm(ks[9], (DEPTH, HEADS_A, CHUNK, CHUNK), CHUNK ** -0.5),
        "b_s": 1.0 + 0.1 * jax.random.normal(ks[10], (DEPTH, HEADS_A, CHUNK), f32),
        "w_pool": nrm(ks[11], (DEPTH, POOL_GROUPS, POOL_GROUP_DIM, POOL_GROUP_DIM), POOL_GROUP_DIM ** -0.5),
        "s_pool": 1.0 + 0.1 * jax.random.normal(ks[12], (DEPTH, WIDTH_B), f32),
        "w_out": nrm(ks[13], (DEPTH, MIX_WIDTH, D_MODEL), MIX_WIDTH ** -0.5),
        "g_mem": gain(ks[14], (D_MODEL,)),
        "g_xattn": gain(ks[15], (DEPTH, D_MODEL)),
        "w_xq": nrm(ks[16], (DEPTH, D_MODEL, D_MODEL), D_MODEL ** -0.5),
        "w_xk": nrm(ks[17], (DEPTH, D_MODEL, D_MODEL), D_MODEL ** -0.5),
        "w_xv": nrm(ks[18], (DEPTH, D_MODEL, D_MODEL), D_MODEL ** -0.5),
        "w_xo": nrm(ks[19], (DEPTH, D_MODEL, D_MODEL), D_MODEL ** -0.5),
        "g_ffn": gain(ks[20], (DEPTH, D_MODEL)),
        "w_group": nrm(ks[21], (DEPTH, D_MODEL, N_EXPERT_GROUPS), D_MODEL ** -0.5),
        "b_group": nrm(ks[22], (DEPTH, N_EXPERT_GROUPS), 0.01),
        "w_router": nrm(ks[23], (DEPTH, D_MODEL, N_EXPERTS), D_MODEL ** -0.5),
        "b_router": nrm(ks[24], (DEPTH, N_EXPERTS), 0.01),
        "w_gate": nrm(ks[25], (DEPTH, N_EXPERTS, D_MODEL, D_EXPERT), D_MODEL ** -0.5),
        "w_up": nrm(ks[26], (DEPTH, N_EXPERTS, D_MODEL, D_EXPERT), D_MODEL ** -0.5),
        "w_down": nrm(ks[27], (DEPTH, N_EXPERTS, D_EXPERT, D_MODEL), D_EXPERT ** -0.5),
        "g_final": gain(ks[28], (D_MODEL,)),
    }


def reference(x_prompt, x_sample, cache_mem_k, cache_mem_v, state_pool, mem_prompt,
              g_mix, w_in, g_v, w_s, b_s, w_pool, s_pool, w_out,
              g_mem, g_xattn, w_xq, w_xk, w_xv, w_xo,
              g_ffn, w_group, b_group, w_router, b_router, w_gate, w_up, w_down, g_final):
    bp = x_prompt.shape[0]
    mem_n = rmsnorm(mem_prompt, g_mem)
    pool_zero = jnp.zeros((bp, POOL_HIST, WIDTH_B), x_prompt.dtype)
    yp, ys = x_prompt, x_sample
    mk_p, mv_p, cv_p, cv_s, pl_p, pl_s = [], [], [], [], [], []
    for l in range(DEPTH):
        mk = (mem_n @ w_xk[l]).reshape(bp, N_MEM, X_HEADS, X_HEAD_DIM)
        mv = (mem_n @ w_xv[l]).reshape(bp, N_MEM, X_HEADS, X_HEAD_DIM)
        wl = (g_mix[l], w_in[l], g_v[l], w_s[l], b_s[l], w_pool[l], s_pool[l], w_out[l],
              g_xattn[l], w_xq[l], w_xo[l],
              g_ffn[l], w_group[l], b_group[l], w_router[l], b_router[l], w_gate[l], w_up[l], w_down[l])
        yp, cvp, plp = layer_fwd(yp, 0, pool_zero, mk, mv, *wl)
        ys, cvs, pls = layer_fwd(ys, PAST_LEN, state_pool[l], cache_mem_k[l], cache_mem_v[l], *wl)
        mk_p.append(mk)
        mv_p.append(mv)
        cv_p.append(cvp)
        cv_s.append(cvs)
        pl_p.append(plp)
        pl_s.append(pls)
    y_prompt = rmsnorm(yp, g_final)
    y_sample = rmsnorm(ys, g_final)
    return (y_prompt, y_sample, jnp.stack(mk_p), jnp.stack(mv_p), jnp.stack(cv_p), jnp.stack(cv_s),
            jnp.stack(pl_p), jnp.stack(pl_s))
```

```python
import functools

import jax
import jax.numpy as jnp
from jax import lax
from jax.experimental import pallas as pl
from jax.experimental.pallas import tpu as pltpu

D_MODEL = 1024
DEPTH = 4
CHUNK = 128
WIDTH_A = 512
WIDTH_B = 512
HEADS_A = 8
HEAD_DIM_A = 64
POOL_WINDOWS = (2, 4, 8, 16)
POOL_GROUP_DIM = 128
POOL_HIST = 15
HALO = 16
PROJ_WIDTH = 2 * WIDTH_A + WIDTH_B
N_MEM = 256
X_HEADS = 4
X_HEAD_DIM = 256
N_GROUPS = 4
PER_GROUP = 4
N_EXPERTS = 16
D_EXPERT = 256
EPS = 1e-6
ROUTER_LANES = 128

F32 = jnp.float32
BF16 = jnp.bfloat16

VMEM_LIMIT_BYTES = 56 * 1024 * 1024


def _rmsnorm(x, g):
    r = lax.rsqrt(jnp.mean(x * x, axis=-1, keepdims=True) + EPS)
    return x * r * g


def _gelu_tanh(x):
    c = 0.7978845608028654
    return 0.5 * x * (1.0 + jnp.tanh(c * (x + 0.044715 * (x * x * x))))


def _silu(x):
    return x * (1.0 / (1.0 + jnp.exp(-x)))


def _dot(a, b):
    return jnp.dot(a, b, preferred_element_type=F32)


def _memkv_kernel(mem_ref, g_ref, wk_ref, wv_ref, k_ref, v_ref, kb_ref, vb_ref):
    h = _rmsnorm(mem_ref[...], g_ref[...]).astype(BF16)
    k = _dot(h, wk_ref[0])
    v = _dot(h, wv_ref[0])
    k_ref[0] = k
    v_ref[0] = v
    kb_ref[0] = k.astype(BF16)
    vb_ref[0] = v.astype(BF16)


def _memkv(mem2d, g_mem, wk, wv):
    rows = mem2d.shape[0]
    tr = 512
    out_sds = jax.ShapeDtypeStruct((DEPTH, rows, D_MODEL), F32)
    outb_sds = jax.ShapeDtypeStruct((DEPTH, rows, D_MODEL), BF16)
    w_spec = pl.BlockSpec((1, D_MODEL, D_MODEL), lambda l, r: (l, 0, 0))
    o_spec = pl.BlockSpec((1, tr, D_MODEL), lambda l, r: (l, r, 0))
    return pl.pallas_call(
        _memkv_kernel,
        grid=(DEPTH, rows // tr),
        in_specs=[pl.BlockSpec((tr, D_MODEL), lambda l, r: (r, 0)),
                  pl.BlockSpec((1, D_MODEL), lambda l, r: (0, 0)),
                  w_spec, w_spec],
        out_specs=[o_spec, o_spec, o_spec, o_spec],
        out_shape=[out_sds, out_sds, outb_sds, outb_sds],
        compiler_params=pltpu.CompilerParams(dimension_semantics=("arbitrary", "arbitrary")),
        name="memkv",
    )(mem2d, g_mem, wk, wv)


def _mix_kernel(*refs, s_t, l_t, pos0, has_hist):
    if has_hist:
        (x_ref, hist_ref, gmix_ref, win_ref, gv_ref, mm_ref, mb_ref, wpool_ref, spool_ref, wout_ref,
         x1_ref, v_ref, zt_ref, zext_ref) = refs
    else:
        (x_ref, gmix_ref, win_ref, gv_ref, mm_ref, mb_ref, wpool_ref, spool_ref, wout_ref,
         x1_ref, v_ref, zt_ref, zext_ref, carry_ref) = refs
    tm = s_t * l_t
    blk = min(l_t, CHUNK)
    j = pl.program_id(1)

    x = x_ref[...].reshape(tm, D_MODEL)
    h = _rmsnorm(x, gmix_ref[...]).astype(BF16)
    proj = _dot(h, win_ref[...])
    ua = _gelu_tanh(proj[:, :2 * WIDTH_A])
    u = ua[:, :WIDTH_A]
    v = _rmsnorm(ua[:, WIDTH_A:], gv_ref[...])
    z = proj[:, 2 * WIDTH_A:]

    r_i = lax.broadcasted_iota(jnp.int32, (tm, tm), 0)
    c_i = lax.broadcasted_iota(jnp.int32, (tm, tm), 1)
    keep = jnp.logical_and(r_i // blk == c_i // blk, c_i <= r_i)
    lane = lax.broadcasted_iota(jnp.int32, (tm, 128), 1)
    vb = v.astype(BF16)
    zero_b = jnp.zeros((), BF16)
    mixed = []
    for p in range(HEADS_A // 2):
        vp = vb[:, 128 * p:128 * (p + 1)]
        lo = jnp.where(lane < HEAD_DIM_A, vp, zero_b)
        hi = jnp.where(lane >= HEAD_DIM_A, vp, zero_b)
        m_lo = jnp.where(keep, mm_ref[2 * p], zero_b)
        m_hi = jnp.where(keep, mm_ref[2 * p + 1], zero_b)
        mixed.append(_dot(m_lo, lo) + _dot(m_hi, hi))
    mixed = jnp.concatenate(mixed, axis=-1) + mb_ref[...]
    a_out = u * mixed

    z3 = z.reshape(s_t, l_t, WIDTH_B)
    if has_hist:
        zext_ref[:, 0:HALO, :] = hist_ref[...]
    else:
        @pl.when(j == 0)
        def _():
            zext_ref[:, 0:HALO, :] = jnp.zeros((s_t, HALO, WIDTH_B), F32)

        @pl.when(j > 0)
        def _():
            zext_ref[:, 0:HALO, :] = carry_ref[...]
    zext_ref[:, HALO:HALO + l_t, :] = z3
    if not has_hist:
        carry_ref[...] = z3[:, l_t - HALO:, :]
    pos = pos0 + j * l_t + lax.broadcasted_iota(jnp.int32, (1, l_t, 1), 1)
    pooled_out = []
    for g, w in enumerate(POOL_WINDOWS):
        sl = slice(g * POOL_GROUP_DIM, (g + 1) * POOL_GROUP_DIM)
        acc = zext_ref[:, HALO:HALO + l_t, sl]
        for k in range(1, w):
            acc = acc + zext_ref[:, HALO - k:HALO - k + l_t, sl]
        cnt = jnp.minimum(w, pos + 1).astype(F32)
        pooled = acc / cnt - z3[:, :, sl]
        pooled_out.append(_dot(pooled.reshape(tm, POOL_GROUP_DIM).astype(BF16), wpool_ref[g]))
    b_out = jnp.concatenate(pooled_out, axis=-1) * spool_ref[...]

    cat = jnp.concatenate([a_out, b_out], axis=-1).astype(BF16)
    x1 = x + _dot(cat, wout_ref[...])
    x1_ref[...] = x1.reshape(s_t, l_t, D_MODEL)
    zt_ref[...] = zext_ref[:, l_t + 1:l_t + HALO, :]
    if has_hist:
        v_ref[...] = v.reshape(s_t, l_t, WIDTH_A)
    else:
        v_ref[...] = v[tm - CHUNK:, :].reshape(1, CHUNK, WIDTH_A)


def _const_spec(shape):
    nd = len(shape)
    return pl.BlockSpec(shape, lambda b, j: (0,) * nd)


def _mix(x3, hist, wts, *, s_t, l_t, pos0):
    s_all, l_all, _ = x3.shape
    has_hist = hist is not None
    tm = s_t * l_t
    grid = (s_all // s_t, l_all // l_t)
    x_spec = pl.BlockSpec((s_t, l_t, D_MODEL), lambda b, j: (b, j, 0))
    in_specs = [x_spec]
    args = [x3]
    if has_hist:
        in_specs.append(pl.BlockSpec((s_t, HALO, WIDTH_B), lambda b, j: (b, 0, 0)))
        args.append(hist)
    for name in ("g_mix", "w_in", "g_v", "mixmat", "mixbias", "w_pool", "s_pool", "w_out"):
        in_specs.append(_const_spec(wts[name].shape))
        args.append(wts[name])
    if has_hist:
        v_shape, v_blk = (s_all, l_all, WIDTH_A), (s_t, l_t, WIDTH_A)
    else:
        v_shape, v_blk = (s_all, CHUNK, WIDTH_A), (1, CHUNK, WIDTH_A)
    out_shape = [jax.ShapeDtypeStruct(x3.shape, F32),
                 jax.ShapeDtypeStruct(v_shape, F32),
                 jax.ShapeDtypeStruct((s_all, POOL_HIST, WIDTH_B), F32)]
    out_specs = [x_spec,
                 pl.BlockSpec(v_blk, lambda b, j: (b, 0, 0)),
                 pl.BlockSpec((s_t, POOL_HIST, WIDTH_B), lambda b, j: (b, 0, 0))]
    scratch = [pltpu.VMEM((s_t, HALO + l_t, WIDTH_B), F32)]
    if not has_hist:
        scratch.append(pltpu.VMEM((s_t, HALO, WIDTH_B), F32))
    return pl.pallas_call(
        functools.partial(_mix_kernel, s_t=s_t, l_t=l_t, pos0=pos0, has_hist=has_hist),
        grid=grid, in_specs=in_specs, out_specs=out_specs, out_shape=out_shape,
        scratch_shapes=scratch,
        compiler_params=pltpu.CompilerParams(dimension_semantics=("arbitrary", "arbitrary"),
                                             vmem_limit_bytes=VMEM_LIMIT_BYTES),
        name="mix_sample" if has_hist else "mix_prompt",
    )(*args)


def _xattn_kernel(x_ref, k_ref, v_ref, g_ref, wq_ref, wo_ref, o_ref, *, s_t, l_t):
    tm = s_t * l_t
    x = x_ref[...].reshape(tm, D_MODEL)
    h = _rmsnorm(x, g_ref[...]).astype(BF16)
    q = _dot(h, wq_ref[...]) * (X_HEAD_DIM ** -0.5)
    outs = []
    for s in range(s_t):
        qs = q[s * l_t:(s + 1) * l_t].astype(BF16)
        heads = []
        for hh in range(X_HEADS):
            sl = slice(hh * X_HEAD_DIM, (hh + 1) * X_HEAD_DIM)
            kh = k_ref[s, :, sl].astype(BF16)
            vh = v_ref[s, :, sl].astype(BF16)
            sc = lax.dot_general(qs[:, sl], kh, (((1,), (1,)), ((), ())), preferred_element_type=F32)
            m = jnp.max(sc, axis=-1, keepdims=True)
            e = jnp.exp(sc - m)
            p = e * (1.0 / jnp.sum(e, axis=-1, keepdims=True))
            heads.append(_dot(p.astype(BF16), vh))
        outs.append(jnp.concatenate(heads, axis=-1))
    o = outs[0] if s_t == 1 else jnp.concatenate(outs, axis=0)
    x2 = x + _dot(o.astype(BF16), wo_ref[...])
    o_ref[...] = x2.reshape(s_t, l_t, D_MODEL)


def _xattn(x3, mem_k, mem_v, mem_off, wts, *, s_t, l_t):
    s_all, l_all, _ = x3.shape
    grid = (s_all // s_t, l_all // l_t)
    x_spec = pl.BlockSpec((s_t, l_t, D_MODEL), lambda b, j: (b, j, 0))
    blk_off = mem_off // s_t
    m_spec = pl.BlockSpec((s_t, N_MEM, D_MODEL), lambda b, j: (blk_off + b, 0, 0))
    return pl.pallas_call(
        functools.partial(_xattn_kernel, s_t=s_t, l_t=l_t),
        grid=grid,
        in_specs=[x_spec, m_spec, m_spec, _const_spec(wts["g_xattn"].shape),
                  _const_spec(wts["w_xq"].shape), _const_spec(wts["w_xo"].shape)],
        out_specs=x_spec,
        out_shape=jax.ShapeDtypeStruct(x3.shape, F32),
        compiler_params=pltpu.CompilerParams(dimension_semantics=("arbitrary", "arbitrary"),
                                             vmem_limit_bytes=VMEM_LIMIT_BYTES),
        name="xattn",
    )(x3, mem_k, mem_v, wts["g_xattn"], wts["w_xq"], wts["w_xo"])


def _route(logits):
    lane = lax.broadcasted_iota(jnp.int32, logits.shape, 1)
    neg = jnp.float32(-jnp.inf)
    big = jnp.int32(1 << 20)
    gmask = lane < N_GROUPS
    gl = jnp.where(gmask, logits, neg)
    gmax = jnp.max(gl, axis=-1, keepdims=True)
    gsum = jnp.sum(jnp.where(gmask, jnp.exp(gl - gmax), 0.0), axis=-1, keepdims=True)
    g_p = 1.0 / gsum
    g_idx = jnp.min(jnp.where(gl == gmax, lane, big), axis=-1, keepdims=True)
    lo = N_GROUPS + PER_GROUP * g_idx
    emask = jnp.logical_and(lane >= lo, lane < lo + PER_GROUP)
    el = jnp.where(emask, logits, neg)
    v1 = jnp.max(el, axis=-1, keepdims=True)
    i1 = jnp.min(jnp.where(el == v1, lane, big), axis=-1, keepdims=True)
    el2 = jnp.where(lane == i1, neg, el)
    v2 = jnp.max(el2, axis=-1, keepdims=True)
    i2 = jnp.min(jnp.where(el2 == v2, lane, big), axis=-1, keepdims=True)
    t = jnp.exp(v2 - v1)
    w1 = g_p / (1.0 + t)
    w2 = g_p * t / (1.0 + t)
    return jnp.where(lane == i1, w1, 0.0) + jnp.where(lane == i2, w2, 0.0)


def _moe_kernel(x_ref, g_ref, wr_ref, br_ref, wg_ref, wu_ref, wd_ref, gf_ref, o_ref, *, s_t, l_t, final):
    tm = s_t * l_t
    x = x_ref[...].reshape(tm, D_MODEL)
    h = _rmsnorm(x, g_ref[...]).astype(BF16)
    logits = _dot(h, wr_ref[...]) + br_ref[...]
    gates = _route(logits)
    acc = jnp.zeros((tm, D_MODEL), F32)
    for e in range(N_EXPERTS):
        hg = _dot(h, wg_ref[e])
        hu = _dot(h, wu_ref[e])
        ge = gates[:, N_GROUPS + e:N_GROUPS + e + 1]
        a = (_silu(hg) * hu * ge).astype(BF16)
        acc = acc + _dot(a, wd_ref[e])
    y = x + acc
    if final:
        y = _rmsnorm(y, gf_ref[...])
    o_ref[...] = y.reshape(s_t, l_t, D_MODEL)


def _resident_spec(shape):
    nd = len(shape)
    return pl.BlockSpec(shape, lambda b, j: (0,) * nd, pipeline_mode=pl.Buffered(1))


def _moe(x3, wts, g_final, *, s_t, l_t, final):
    s_all, l_all, _ = x3.shape
    grid = (s_all // s_t, l_all // l_t)
    x_spec = pl.BlockSpec((s_t, l_t, D_MODEL), lambda b, j: (b, j, 0))
    names = ("g_ffn", "w_r", "b_r", "w_gate", "w_up", "w_down")
    return pl.pallas_call(
        functools.partial(_moe_kernel, s_t=s_t, l_t=l_t, final=final),
        grid=grid,
        in_specs=[x_spec] + [_resident_spec(wts[n].shape) for n in names] + [_const_spec(g_final.shape)],
        out_specs=x_spec,
        out_shape=jax.ShapeDtypeStruct(x3.shape, F32),
        compiler_params=pltpu.CompilerParams(dimension_semantics=("arbitrary", "arbitrary"),
                                             vmem_limit_bytes=VMEM_LIMIT_BYTES),
        name="moe",
    )(x3, *[wts[n] for n in names], g_final)


def _mix_matrices(w_s_l, b_s_l, blk, tm):
    reps = tm // blk
    mm = jnp.tile(w_s_l[:, :blk, :blk], (1, reps, reps)).astype(BF16)
    bias = jnp.repeat(b_s_l[:, :blk].T, HEAD_DIM_A, axis=1)
    return mm, jnp.tile(bias, (reps, 1))


def kernel(x_prompt, x_sample, cache_mem_k, cache_mem_v, state_pool, mem_prompt, g_mix, w_in, g_v, w_s, b_s, w_pool, s_pool, w_out, g_mem, g_xattn, w_xq, w_xk, w_xv, w_xo, g_ffn, w_group, b_group, w_router, b_router, w_gate, w_up, w_down, g_final):
    bp, seq, _ = x_prompt.shape
    bs, dseq, _ = x_sample.shape
    past_len = 16384

    mem_k, mem_v, mem_kb, mem_vb = _memkv(mem_prompt.reshape(bp * N_MEM, D_MODEL), g_mem.reshape(1, D_MODEL),
                                          w_xk.astype(BF16), w_xv.astype(BF16))
    mem_kb = mem_kb.reshape(DEPTH * bp, N_MEM, D_MODEL)
    mem_vb = mem_vb.reshape(DEPTH * bp, N_MEM, D_MODEL)
    cache_k = cache_mem_k.reshape(DEPTH * bs, N_MEM, D_MODEL)
    cache_v = cache_mem_v.reshape(DEPTH * bs, N_MEM, D_MODEL)
    hist_s = jnp.pad(state_pool, ((0, 0), (0, 0), (HALO - POOL_HIST, 0), (0, 0)))

    tm = 256
    ls_p, ss_s = tm, tm // dseq
    g_final2 = g_final.reshape(1, D_MODEL)
    w_r = jnp.concatenate([w_group, w_router, jnp.zeros((DEPTH, D_MODEL, ROUTER_LANES - N_GROUPS - N_EXPERTS), F32)],
                          axis=-1).astype(BF16)
    b_r = jnp.concatenate([b_group, b_router, jnp.zeros((DEPTH, ROUTER_LANES - N_GROUPS - N_EXPERTS), F32)], axis=-1)

    yp, ys = x_prompt, x_sample
    cv_p, cv_s, pl_p, pl_s = [], [], [], []
    for l in range(DEPTH):
        common = {
            "g_mix": g_mix[l].reshape(1, D_MODEL), "w_in": w_in[l].astype(BF16), "g_v": g_v[l].reshape(1, WIDTH_A),
            "w_pool": w_pool[l].astype(BF16), "s_pool": s_pool[l].reshape(1, WIDTH_B), "w_out": w_out[l].astype(BF16),
            "g_xattn": g_xattn[l].reshape(1, D_MODEL), "w_xq": w_xq[l].astype(BF16), "w_xo": w_xo[l].astype(BF16),
            "g_ffn": g_ffn[l].reshape(1, D_MODEL), "w_r": w_r[l], "b_r": b_r[l].reshape(1, ROUTER_LANES),
            "w_gate": w_gate[l].astype(BF16), "w_up": w_up[l].astype(BF16), "w_down": w_down[l].astype(BF16),
        }
        mm_p, mb_p = _mix_matrices(w_s[l], b_s[l], CHUNK, tm)
        mm_s, mb_s = _mix_matrices(w_s[l], b_s[l], dseq, tm)
        wp = dict(common, mixmat=mm_p, mixbias=mb_p)
        ws = dict(common, mixmat=mm_s, mixbias=mb_s)
        final = l == DEPTH - 1

        yp, cvp, plp = _mix(yp, None, wp, s_t=1, l_t=ls_p, pos0=0)
        yp = _xattn(yp, mem_kb, mem_vb, l * bp, wp, s_t=1, l_t=ls_p)
        yp = _moe(yp, wp, g_final2, s_t=1, l_t=ls_p, final=final)

        ys, cvs, pls = _mix(ys, hist_s[l], ws, s_t=ss_s, l_t=dseq, pos0=past_len)
        ys = _xattn(ys, cache_k, cache_v, l * bs, ws, s_t=8, l_t=dseq)
        ys = _moe(ys, ws, g_final2, s_t=ss_s, l_t=dseq, final=final)

        cv_p.append(cvp.reshape(bp, CHUNK, HEADS_A, HEAD_DIM_A))
        cv_s.append(cvs.reshape(bs, dseq, HEADS_A, HEAD_DIM_A))
        pl_p.append(plp)
        pl_s.append(pls)

    return (yp, ys,
            mem_k.reshape(DEPTH, bp, N_MEM, X_HEADS, X_HEAD_DIM),
            mem_v.reshape(DEPTH, bp, N_MEM, X_HEADS, X_HEAD_DIM),
            jnp.stack(cv_p), jnp.stack(cv_s), jnp.stack(pl_p), jnp.stack(pl_s))
```

```python
import functools

import jax
import jax.numpy as jnp
from jax import lax
from jax.experimental import pallas as pl
from jax.experimental.pallas import tpu as pltpu

D_MODEL = 1024
DEPTH = 4
PAST_LEN = 16384
CHUNK = 128
WIDTH_A = 512
WIDTH_B = 512
HEADS_A = 8
HEAD_DIM_A = 64
POOL_WINDOWS = (2, 4, 8, 16)
POOL_GROUP_DIM = 128
POOL_HIST = 15
HALO = 16
PROJ_WIDTH = 2 * WIDTH_A + WIDTH_B
N_MEM = 256
X_HEADS = 4
X_HEAD_DIM = 256
N_GROUPS = 4
PER_GROUP = 4
N_EXPERTS = 16
D_EXPERT = 256
EPS = 1e-6
ROUTER_LANES = 128
LANES = 128
GATE_ROWS = 16

MOE_TM = 512
MOE_BM = 128
MOE_R = MOE_TM + N_GROUPS * MOE_BM

F32 = jnp.float32
BF16 = jnp.bfloat16

VMEM_LIMIT_BYTES = 56 * 1024 * 1024


def _rmsnorm(x, g):
    r = lax.rsqrt(jnp.mean(x * x, axis=-1, keepdims=True) + EPS)
    return x * r * g


def _gelu_tanh(x):
    c = 0.7978845608028654
    return 0.5 * x * (1.0 + jnp.tanh(c * (x + 0.044715 * (x * x * x))))


def _silu(x):
    return x * (1.0 / (1.0 + jnp.exp(-x)))


def _dot(a, b):
    return jnp.dot(a, b, preferred_element_type=F32)


def _dot_nt(a, b):
    return lax.dot_general(a, b, (((1,), (1,)), ((), ())), preferred_element_type=F32)


def _memkv_kernel(mem_ref, g_ref, wk_ref, wv_ref, k_ref, v_ref, kb_ref, vb_ref):
    h = _rmsnorm(mem_ref[...], g_ref[...]).astype(BF16)
    k = _dot(h, wk_ref[0])
    v = _dot(h, wv_ref[0])
    for hh in range(X_HEADS):
        sl = slice(hh * X_HEAD_DIM, (hh + 1) * X_HEAD_DIM)
        k_ref[0, 0, :, hh, :] = k[:, sl]
        v_ref[0, 0, :, hh, :] = v[:, sl]
    kb_ref[0] = k.astype(BF16)
    vb_ref[0] = v.astype(BF16)


def _memkv(mem2d, g_mem, wk, wv):
    rows = mem2d.shape[0]
    out_sds = jax.ShapeDtypeStruct((DEPTH, rows // N_MEM, N_MEM, X_HEADS, X_HEAD_DIM), F32)
    outb_sds = jax.ShapeDtypeStruct((DEPTH, rows, D_MODEL), BF16)
    w_spec = pl.BlockSpec((1, D_MODEL, D_MODEL), lambda l, r: (l, 0, 0))
    o5_spec = pl.BlockSpec((1, 1, N_MEM, X_HEADS, X_HEAD_DIM), lambda l, r: (l, r, 0, 0, 0))
    o_spec = pl.BlockSpec((1, N_MEM, D_MODEL), lambda l, r: (l, r, 0))
    return pl.pallas_call(
        _memkv_kernel,
        grid=(DEPTH, rows // N_MEM),
        in_specs=[pl.BlockSpec((N_MEM, D_MODEL), lambda l, r: (r, 0)),
                  pl.BlockSpec((1, D_MODEL), lambda l, r: (0, 0)),
                  w_spec, w_spec],
        out_specs=[o5_spec, o5_spec, o_spec, o_spec],
        out_shape=[out_sds, out_sds, outb_sds, outb_sds],
        compiler_params=pltpu.CompilerParams(dimension_semantics=("arbitrary", "arbitrary")),
        name="memkv",
    )(mem2d, g_mem, wk, wv)


def _mix_kernel(*refs, s_t, l_t, pos0, has_hist):
    if has_hist:
        (x_ref, hist_ref, gmix_ref, win_ref, gv_ref, mm_ref, mb_ref, wpool_ref, spool_ref, wout_ref,
         x1_ref, v_ref, zt_ref, zext_ref) = refs
    else:
        (x_ref, gmix_ref, win_ref, gv_ref, mm_ref, mb_ref, wpool_ref, spool_ref, wout_ref,
         x1_ref, v_ref, zt_ref, zext_ref, carry_ref) = refs
    tm = s_t * l_t
    blk = min(l_t, CHUNK)
    j = pl.program_id(1)

    x = x_ref[...].reshape(tm, D_MODEL)
    h = _rmsnorm(x, gmix_ref[...]).astype(BF16)
    proj = _dot(h, win_ref[...])
    ua = _gelu_tanh(proj[:, :2 * WIDTH_A])
    u = ua[:, :WIDTH_A]
    v = _rmsnorm(ua[:, WIDTH_A:], gv_ref[...])
    z = proj[:, 2 * WIDTH_A:]

    r_i = lax.broadcasted_iota(jnp.int32, (tm, tm), 0)
    c_i = lax.broadcasted_iota(jnp.int32, (tm, tm), 1)
    keep = jnp.logical_and(r_i // blk == c_i // blk, c_i <= r_i)
    lane = lax.broadcasted_iota(jnp.int32, (tm, LANES), 1)
    vb = v.astype(BF16)
    zero_b = jnp.zeros((), BF16)
    mixed = []
    for p in range(HEADS_A // 2):
        vp = vb[:, LANES * p:LANES * (p + 1)]
        lo = jnp.where(lane < HEAD_DIM_A, vp, zero_b)
        hi = jnp.where(lane >= HEAD_DIM_A, vp, zero_b)
        m_lo = jnp.where(keep, mm_ref[2 * p], zero_b)
        m_hi = jnp.where(keep, mm_ref[2 * p + 1], zero_b)
        mixed.append(_dot(m_lo, lo) + _dot(m_hi, hi))
    mixed = jnp.concatenate(mixed, axis=-1) + mb_ref[...]
    a_out = u * mixed

    z3 = z.reshape(s_t, l_t, WIDTH_B)
    if has_hist:
        zext_ref[:, 0:HALO, :] = hist_ref[...]
    else:
        @pl.when(j == 0)
        def _():
            zext_ref[:, 0:HALO, :] = jnp.zeros((s_t, HALO, WIDTH_B), F32)

        @pl.when(j > 0)
        def _():
            zext_ref[:, 0:HALO, :] = carry_ref[...]
    zext_ref[:, HALO:HALO + l_t, :] = z3
    if not has_hist:
        carry_ref[...] = z3[:, l_t - HALO:, :]
    pos = pos0 + j * l_t + lax.broadcasted_iota(jnp.int32, (1, l_t, 1), 1)
    pooled_out = []
    for g, w in enumerate(POOL_WINDOWS):
        sl = slice(g * POOL_GROUP_DIM, (g + 1) * POOL_GROUP_DIM)
        acc = zext_ref[:, HALO:HALO + l_t, sl]
        for k in range(1, w):
            acc = acc + zext_ref[:, HALO - k:HALO - k + l_t, sl]
        cnt = jnp.minimum(w, pos + 1).astype(F32)
        pooled = acc / cnt - z3[:, :, sl]
        pooled_out.append(_dot(pooled.reshape(tm, POOL_GROUP_DIM).astype(BF16), wpool_ref[g]))
    b_out = jnp.concatenate(pooled_out, axis=-1) * spool_ref[...]

    cat = jnp.concatenate([a_out, b_out], axis=-1).astype(BF16)
    x1 = x + _dot(cat, wout_ref[...])
    x1_ref[...] = x1.reshape(s_t, l_t, D_MODEL)
    zt_ref[...] = zext_ref[:, l_t + 1:l_t + HALO, :]
    if has_hist:
        v_ref[...] = v.reshape(s_t, l_t, WIDTH_A)
    else:
        v_ref[...] = v[tm - CHUNK:, :].reshape(1, CHUNK, WIDTH_A)


def _const_spec(shape):
    nd = len(shape)
    return pl.BlockSpec(shape, lambda b, j: (0,) * nd)


def _mix(x3, hist, wts, *, s_t, l_t, pos0):
    s_all, l_all, _ = x3.shape
    has_hist = hist is not None
    grid = (s_all // s_t, l_all // l_t)
    x_spec = pl.BlockSpec((s_t, l_t, D_MODEL), lambda b, j: (b, j, 0))
    in_specs = [x_spec]
    args = [x3]
    if has_hist:
        in_specs.append(pl.BlockSpec((s_t, HALO, WIDTH_B), lambda b, j: (b, 0, 0)))
        args.append(hist)
    for name in ("g_mix", "w_in", "g_v", "mixmat", "mixbias", "w_pool", "s_pool", "w_out"):
        in_specs.append(_const_spec(wts[name].shape))
        args.append(wts[name])
    if has_hist:
        v_shape, v_blk = (s_all, l_all, WIDTH_A), (s_t, l_t, WIDTH_A)
    else:
        v_shape, v_blk = (s_all, CHUNK, WIDTH_A), (1, CHUNK, WIDTH_A)
    out_shape = [jax.ShapeDtypeStruct(x3.shape, F32),
                 jax.ShapeDtypeStruct(v_shape, F32),
                 jax.ShapeDtypeStruct((s_all, POOL_HIST, WIDTH_B), F32)]
    out_specs = [x_spec,
                 pl.BlockSpec(v_blk, lambda b, j: (b, 0, 0)),
                 pl.BlockSpec((s_t, POOL_HIST, WIDTH_B), lambda b, j: (b, 0, 0))]
    scratch = [pltpu.VMEM((s_t, HALO + l_t, WIDTH_B), F32)]
    if not has_hist:
        scratch.append(pltpu.VMEM((s_t, HALO, WIDTH_B), F32))
    return pl.pallas_call(
        functools.partial(_mix_kernel, s_t=s_t, l_t=l_t, pos0=pos0, has_hist=has_hist),
        grid=grid, in_specs=in_specs, out_specs=out_specs, out_shape=out_shape,
        scratch_shapes=scratch,
        compiler_params=pltpu.CompilerParams(dimension_semantics=("arbitrary", "arbitrary"),
                                             vmem_limit_bytes=VMEM_LIMIT_BYTES),
        name="mix_sample" if has_hist else "mix_prompt",
    )(*args)


def _xattn_kernel(x_ref, k_ref, v_ref, g_ref, wq_ref, wo_ref, o_ref, *, s_t, l_t, mem5d):
    tm = s_t * l_t
    x = x_ref[...].reshape(tm, D_MODEL)
    h = _rmsnorm(x, g_ref[...]).astype(BF16)
    q = _dot(h, wq_ref[...]) * (X_HEAD_DIM ** -0.5)
    outs = []
    for s in range(s_t):
        qs = q[s * l_t:(s + 1) * l_t].astype(BF16)
        heads = []
        for hh in range(X_HEADS):
            sl = slice(hh * X_HEAD_DIM, (hh + 1) * X_HEAD_DIM)
            if mem5d:
                kh = k_ref[0, s, :, hh, :].astype(BF16)
                vh = v_ref[0, s, :, hh, :].astype(BF16)
            else:
                kh = k_ref[s, :, sl]
                vh = v_ref[s, :, sl]
            sc = _dot_nt(qs[:, sl], kh)
            m = jnp.max(sc, axis=-1, keepdims=True)
            e = jnp.exp(sc - m)
            p = e * (1.0 / jnp.sum(e, axis=-1, keepdims=True))
            heads.append(_dot(p.astype(BF16), vh))
        outs.append(jnp.concatenate(heads, axis=-1))
    o = outs[0] if s_t == 1 else jnp.concatenate(outs, axis=0)
    x2 = x + _dot(o.astype(BF16), wo_ref[...])
    o_ref[...] = x2.reshape(s_t, l_t, D_MODEL)


def _xattn(x3, mem_k, mem_v, layer, wts, *, s_t, l_t):
    s_all, l_all, _ = x3.shape
    grid = (s_all // s_t, l_all // l_t)
    x_spec = pl.BlockSpec((s_t, l_t, D_MODEL), lambda b, j: (b, j, 0))
    mem5d = mem_k.ndim == 5
    if mem5d:
        m_spec = pl.BlockSpec((1, s_t, N_MEM, X_HEADS, X_HEAD_DIM), lambda b, j: (layer, b, 0, 0, 0))
    else:
        blk_off = layer * (s_all // s_t)
        m_spec = pl.BlockSpec((s_t, N_MEM, D_MODEL), lambda b, j: (blk_off + b, 0, 0))
    return pl.pallas_call(
        functools.partial(_xattn_kernel, s_t=s_t, l_t=l_t, mem5d=mem5d),
        grid=grid,
        in_specs=[x_spec, m_spec, m_spec, _const_spec(wts["g_xattn"].shape),
                  _const_spec(wts["w_xq"].shape), _const_spec(wts["w_xo"].shape)],
        out_specs=x_spec,
        out_shape=jax.ShapeDtypeStruct(x3.shape, F32),
        compiler_params=pltpu.CompilerParams(dimension_semantics=("arbitrary", "arbitrary"),
                                             vmem_limit_bytes=VMEM_LIMIT_BYTES),
        name="xattn_sample" if mem5d else "xattn_prompt",
    )(x3, mem_k, mem_v, wts["g_xattn"], wts["w_xq"], wts["w_xo"])


def _first_index(vals, target):
    idx = jnp.full(target.shape, len(vals) - 1, jnp.int32)
    for i in range(len(vals) - 2, -1, -1):
        idx = jnp.where(vals[i] == target, i, idx)
    return idx


def _max_of(vals):
    m = vals[0]
    for v in vals[1:]:
        m = jnp.maximum(m, v)
    return m


def _route(lt):
    rows = [lt[i:i + 1, :] for i in range(N_GROUPS + N_EXPERTS)]
    g_l = rows[:N_GROUPS]
    gmax = _max_of(g_l)
    gsum = g_l[0] * 0.0
    for v in g_l:
        gsum = gsum + jnp.exp(v - gmax)
    g_p = 1.0 / gsum
    g_idx = _first_index(g_l, gmax)
    e_sel = []
    for j in range(PER_GROUP):
        v = rows[N_GROUPS + PER_GROUP * (N_GROUPS - 1) + j]
        for g in range(N_GROUPS - 2, -1, -1):
            v = jnp.where(g_idx == g, rows[N_GROUPS + PER_GROUP * g + j], v)
        e_sel.append(v)
    v1 = _max_of(e_sel)
    i1 = _first_index(e_sel, v1)
    neg = jnp.float32(-jnp.inf)
    e_rest = [jnp.where(i1 == j, neg, e_sel[j]) for j in range(PER_GROUP)]
    v2 = _max_of(e_rest)
    i2 = _first_index(e_rest, v2)
    t = jnp.exp(v2 - v1)
    w1 = g_p / (1.0 + t)
    w2 = g_p * t / (1.0 + t)
    gates = [jnp.where(i1 == j, w1, 0.0) + jnp.where(i2 == j, w2, 0.0) for j in range(PER_GROUP)]
    return g_idx, gates


def _stack_rows(rows, n_rows, tm):
    ri = lax.broadcasted_iota(jnp.int32, (n_rows, tm), 0)
    out = jnp.zeros((n_rows, tm), F32)
    for i, r in enumerate(rows):
        out = jnp.where(ri == i, r, out)
    return out


def _moe_kernel(x_ref, g_ref, wr_ref, br_ref, wg_ref, wu_ref, wd_ref, gf_ref, o_ref, ys_ref, *, s_t, l_t, final):
    tm = s_t * l_t
    x = x_ref[...].reshape(tm, D_MODEL)
    h = _rmsnorm(x, g_ref[...]).astype(BF16)
    logits = _dot(h, wr_ref[...]) + br_ref[...]
    g_idx, gates = _route(logits.T)

    member = [(g_idx == g).astype(F32) for g in range(N_GROUPS)]
    t_r = lax.broadcasted_iota(jnp.int32, (tm, tm), 0)
    t_c = lax.broadcasted_iota(jnp.int32, (tm, tm), 1)
    upper = jnp.where(t_r <= t_c, 1.0, 0.0).astype(BF16)
    incl = _dot(_stack_rows(member, GATE_ROWS, tm).astype(BF16), upper)
    end_blk = []
    posf = jnp.zeros((1, tm), F32)
    run = jnp.zeros((1, 1), F32)
    for g in range(N_GROUPS):
        n_g = jnp.sum(member[g], axis=-1, keepdims=True)
        posf = posf + member[g] * (run * MOE_BM + incl[g:g + 1, :] - 1.0)
        run = run + jnp.floor((n_g + (MOE_BM - 1)) * (1.0 / MOE_BM))
        end_blk.append(run)
    pos = posf.astype(jnp.int32)
    e0, e1, e2, n_blk = [e.astype(jnp.int32)[0, 0] for e in end_blk]

    g_hi = [gt.astype(BF16).astype(F32) for gt in gates]
    g_lo = [gt - hi for gt, hi in zip(gates, g_hi)]
    gmat = _stack_rows(g_hi + g_lo, GATE_ROWS, tm).astype(BF16)

    r_iota = lax.broadcasted_iota(jnp.int32, (MOE_BM, tm), 0)

    def block(b, carry):
        r0 = pl.multiple_of(b * MOE_BM, MOE_BM)
        grp = (b >= e0).astype(jnp.int32) + (b >= e1).astype(jnp.int32) + (b >= e2).astype(jnp.int32)
        pb = jnp.where(r_iota + r0 == pos, 1.0, 0.0).astype(BF16)
        xb = _dot(pb, h).astype(BF16)
        gs = _dot_nt(pb, gmat)
        acc = jnp.zeros((MOE_BM, D_MODEL), F32)
        for j in range(PER_GROUP):
            e = grp * PER_GROUP + j
            hg = _dot(xb, wg_ref[e])
            hu = _dot(xb, wu_ref[e])
            gj = gs[:, j:j + 1] + gs[:, PER_GROUP + j:PER_GROUP + j + 1]
            a = (_silu(hg) * hu * gj).astype(BF16)
            acc = acc + _dot(a, wd_ref[e])
        ys_ref[pl.ds(r0, MOE_BM), :] = acc.astype(BF16)
        return carry

    lax.fori_loop(0, n_blk, block, 0)

    def clear(b, carry):
        r0 = pl.multiple_of(b * MOE_BM, MOE_BM)
        ys_ref[pl.ds(r0, MOE_BM), :] = jnp.zeros((MOE_BM, D_MODEL), BF16)
        return carry

    lax.fori_loop(n_blk, MOE_R // MOE_BM, clear, 0)

    pos_col = jnp.broadcast_to(posf, (LANES, tm)).T
    lane = lax.broadcasted_iota(jnp.int32, (tm, LANES), 1).astype(F32)
    pt = jnp.concatenate(
        [jnp.where(pos_col == lane + float(LANES * k), 1.0, 0.0).astype(BF16) for k in range(MOE_R // LANES)],
        axis=-1)
    y = x + _dot(pt, ys_ref[...])
    if final:
        y = _rmsnorm(y, gf_ref[...])
    o_ref[...] = y.reshape(s_t, l_t, D_MODEL)


def _resident_spec(shape):
    nd = len(shape)
    return pl.BlockSpec(shape, lambda b, j: (0,) * nd, pipeline_mode=pl.Buffered(1))


def _moe(x3, wts, g_final, *, s_t, l_t, final):
    s_all, l_all, _ = x3.shape
    assert s_t * l_t == MOE_TM
    grid = (s_all // s_t, l_all // l_t)
    x_spec = pl.BlockSpec((s_t, l_t, D_MODEL), lambda b, j: (b, j, 0))
    names = ("g_ffn", "w_r", "b_r", "w_gate", "w_up", "w_down")
    return pl.pallas_call(
        functools.partial(_moe_kernel, s_t=s_t, l_t=l_t, final=final),
        grid=grid,
        in_specs=[x_spec] + [_resident_spec(wts[n].shape) for n in names] + [_const_spec(g_final.shape)],
        out_specs=x_spec,
        out_shape=jax.ShapeDtypeStruct(x3.shape, F32),
        scratch_shapes=[pltpu.VMEM((MOE_R, D_MODEL), BF16)],
        compiler_params=pltpu.CompilerParams(dimension_semantics=("arbitrary", "arbitrary"),
                                             vmem_limit_bytes=VMEM_LIMIT_BYTES),
        name="moe",
    )(x3, *[wts[n] for n in names], g_final)


def _mix_matrices(w_s_l, b_s_l, blk, tm):
    reps = tm // blk
    mm = jnp.tile(w_s_l[:, :blk, :blk], (1, reps, reps)).astype(BF16)
    bias = jnp.repeat(b_s_l[:, :blk].T, HEAD_DIM_A, axis=1)
    return mm, jnp.tile(bias, (reps, 1))


def kernel(x_prompt, x_sample, cache_mem_k, cache_mem_v, state_pool, mem_prompt, g_mix, w_in, g_v, w_s, b_s, w_pool, s_pool, w_out, g_mem, g_xattn, w_xq, w_xk, w_xv, w_xo, g_ffn, w_group, b_group, w_router, b_router, w_gate, w_up, w_down, g_final):
    bp, seq, _ = x_prompt.shape
    bs, dseq, _ = x_sample.shape

    mem_k, mem_v, mem_kb, mem_vb = _memkv(mem_prompt.reshape(bp * N_MEM, D_MODEL), g_mem.reshape(1, D_MODEL),
                                          w_xk.astype(BF16), w_xv.astype(BF16))
    mem_kb = mem_kb.reshape(DEPTH * bp, N_MEM, D_MODEL)
    mem_vb = mem_vb.reshape(DEPTH * bp, N_MEM, D_MODEL)
    hist_s = jnp.pad(state_pool, ((0, 0), (0, 0), (HALO - POOL_HIST, 0), (0, 0)))

    tm = 256
    ls_p, ss_s = tm, tm // dseq
    g_final2 = g_final.reshape(1, D_MODEL)
    w_r = jnp.concatenate([w_group, w_router, jnp.zeros((DEPTH, D_MODEL, ROUTER_LANES - N_GROUPS - N_EXPERTS), F32)],
                          axis=-1).astype(BF16)
    b_r = jnp.concatenate([b_group, b_router, jnp.zeros((DEPTH, ROUTER_LANES - N_GROUPS - N_EXPERTS), F32)], axis=-1)

    yp, ys = x_prompt, x_sample
    cv_p, cv_s, pl_p, pl_s = [], [], [], []
    for l in range(DEPTH):
        common = {
            "g_mix": g_mix[l].reshape(1, D_MODEL), "w_in": w_in[l].astype(BF16), "g_v": g_v[l].reshape(1, WIDTH_A),
            "w_pool": w_pool[l].astype(BF16), "s_pool": s_pool[l].reshape(1, WIDTH_B), "w_out": w_out[l].astype(BF16),
            "g_xattn": g_xattn[l].reshape(1, D_MODEL), "w_xq": w_xq[l].astype(BF16), "w_xo": w_xo[l].astype(BF16),
            "g_ffn": g_ffn[l].reshape(1, D_MODEL), "w_r": w_r[l], "b_r": b_r[l].reshape(1, ROUTER_LANES),
            "w_gate": w_gate[l].astype(BF16), "w_up": w_up[l].astype(BF16), "w_down": w_down[l].astype(BF16),
        }
        mm_p, mb_p = _mix_matrices(w_s[l], b_s[l], CHUNK, tm)
        mm_s, mb_s = _mix_matrices(w_s[l], b_s[l], dseq, tm)
        wp = dict(common, mixmat=mm_p, mixbias=mb_p)
        ws = dict(common, mixmat=mm_s, mixbias=mb_s)
        final = l == DEPTH - 1

        yp, cvp, plp = _mix(yp, None, wp, s_t=1, l_t=ls_p, pos0=0)
        yp = _xattn(yp, mem_kb, mem_vb, l, wp, s_t=1, l_t=ls_p)
        yp = _moe(yp, wp, g_final2, s_t=1, l_t=MOE_TM, final=final)

        ys, cvs, pls = _mix(ys, hist_s[l], ws, s_t=ss_s, l_t=dseq, pos0=PAST_LEN)
        ys = _xattn(ys, cache_mem_k, cache_mem_v, l, ws, s_t=8, l_t=dseq)
        ys = _moe(ys, ws, g_final2, s_t=MOE_TM // dseq, l_t=dseq, final=final)

        cv_p.append(cvp.reshape(bp, CHUNK, HEADS_A, HEAD_DIM_A))
        cv_s.append(cvs.reshape(bs, dseq, HEADS_A, HEAD_DIM_A))
        pl_p.append(plp)
        pl_s.append(pls)

    return (yp, ys, mem_k, mem_v, jnp.stack(cv_p), jnp.stack(cv_s), jnp.stack(pl_p), jnp.stack(pl_s))
```

```python
import functools

import jax
import jax.numpy as jnp
from jax import lax
from jax.experimental import pallas as pl
from jax.experimental.pallas import tpu as pltpu

D_MODEL = 1024
DEPTH = 4
PAST_LEN = 16384
CHUNK = 128
WIDTH_A = 512
WIDTH_B = 512
HEADS_A = 8
HEAD_DIM_A = 64
POOL_WINDOWS = (2, 4, 8, 16)
POOL_GROUP_DIM = 128
POOL_HIST = 15
HALO = 16
PROJ_WIDTH = 2 * WIDTH_A + WIDTH_B
N_MEM = 256
X_HEADS = 4
X_HEAD_DIM = 256
N_GROUPS = 4
PER_GROUP = 4
N_EXPERTS = 16
D_EXPERT = 256
EPS = 1e-6
ROUTER_LANES = 128
LANES = 128
GATE_ROWS = 16

PROMPT_TM = 512
SAMPLE_MIX_SEQS = 32
SAMPLE_ATT_SEQS = 8
MOE_TM = 512
MOE_BM = 128
MOE_R = MOE_TM + N_GROUPS * MOE_BM
MOE_UNROLL = 2

F32 = jnp.float32
BF16 = jnp.bfloat16

VMEM_LIMIT_BYTES = 56 * 1024 * 1024


def _rmsnorm(x, g):
    r = lax.rsqrt(jnp.mean(x * x, axis=-1, keepdims=True) + EPS)
    return x * r * g


def _gelu_tanh(x):
    c = 0.7978845608028654
    return 0.5 * x * (1.0 + jnp.tanh(c * (x + 0.044715 * (x * x * x))))


def _silu(x):
    return x * (1.0 / (1.0 + jnp.exp(-x)))


def _dot(a, b):
    return jnp.dot(a, b, preferred_element_type=F32)


def _dot_nt(a, b):
    return lax.dot_general(a, b, (((1,), (1,)), ((), ())), preferred_element_type=F32)


def _layer_spec(arr, layer, **kw):
    nd = arr.ndim - 1
    return pl.BlockSpec((None,) + arr.shape[1:], lambda b, j: (layer,) + (0,) * nd, **kw)


_PARAMS = pltpu.CompilerParams(dimension_semantics=("arbitrary", "arbitrary"), vmem_limit_bytes=VMEM_LIMIT_BYTES)


def _memkv_kernel(mem_ref, g_ref, wk_ref, wv_ref, k_ref, v_ref, kb_ref, vb_ref):
    h = _rmsnorm(mem_ref[...], g_ref[...]).astype(BF16)
    k = _dot(h, wk_ref[0])
    v = _dot(h, wv_ref[0])
    for hh in range(X_HEADS):
        sl = slice(hh * X_HEAD_DIM, (hh + 1) * X_HEAD_DIM)
        k_ref[0, 0, :, hh, :] = k[:, sl]
        v_ref[0, 0, :, hh, :] = v[:, sl]
    kb_ref[0] = k.astype(BF16)
    vb_ref[0] = v.astype(BF16)


def _memkv(mem2d, g_mem, wk, wv):
    rows = mem2d.shape[0]
    out_sds = jax.ShapeDtypeStruct((DEPTH, rows // N_MEM, N_MEM, X_HEADS, X_HEAD_DIM), F32)
    outb_sds = jax.ShapeDtypeStruct((DEPTH, rows, D_MODEL), BF16)
    w_spec = pl.BlockSpec((1, D_MODEL, D_MODEL), lambda l, r: (l, 0, 0))
    o5_spec = pl.BlockSpec((1, 1, N_MEM, X_HEADS, X_HEAD_DIM), lambda l, r: (l, r, 0, 0, 0))
    o_spec = pl.BlockSpec((1, N_MEM, D_MODEL), lambda l, r: (l, r, 0))
    return pl.pallas_call(
        _memkv_kernel,
        grid=(DEPTH, rows // N_MEM),
        in_specs=[pl.BlockSpec((N_MEM, D_MODEL), lambda l, r: (r, 0)),
                  pl.BlockSpec((1, D_MODEL), lambda l, r: (0, 0)),
                  w_spec, w_spec],
        out_specs=[o5_spec, o5_spec, o_spec, o_spec],
        out_shape=[out_sds, out_sds, outb_sds, outb_sds],
        compiler_params=pltpu.CompilerParams(dimension_semantics=("arbitrary", "arbitrary")),
        name="memkv",
    )(mem2d, g_mem, wk, wv)


def _gate_chunks(v, ws_ref, tm):
    r_i = lax.broadcasted_iota(jnp.int32, (CHUNK, CHUNK), 0)
    c_i = lax.broadcasted_iota(jnp.int32, (CHUNK, CHUNK), 1)
    causal = c_i <= r_i
    lane = lax.broadcasted_iota(jnp.int32, (CHUNK, LANES), 1)
    zero_b = jnp.zeros((), BF16)
    vb = v.astype(BF16)
    pair_w = []
    for p in range(HEADS_A // 2):
        m_lo = jnp.where(causal, ws_ref[2 * p], zero_b)
        m_hi = jnp.where(causal, ws_ref[2 * p + 1], zero_b)
        pair_w.append(jnp.concatenate([m_lo, m_hi], axis=1))
    rows = []
    for c in range(tm // CHUNK):
        cols = []
        for p in range(HEADS_A // 2):
            vp = vb[c * CHUNK:(c + 1) * CHUNK, LANES * p:LANES * (p + 1)]
            lo = jnp.where(lane < HEAD_DIM_A, vp, zero_b)
            hi = jnp.where(lane >= HEAD_DIM_A, vp, zero_b)
            cols.append(_dot(pair_w[p], jnp.concatenate([lo, hi], axis=0)))
        rows.append(jnp.concatenate(cols, axis=-1))
    return jnp.concatenate(rows, axis=0)


def _gate_short(v3, coef_ref, l_t):
    i_i = lax.broadcasted_iota(jnp.int32, (l_t, WIDTH_A), 0)
    acc = None
    for jj in range(l_t):
        cj = jnp.where(i_i >= jj, coef_ref[jj], 0.0)
        term = cj[None, :, :] * v3[:, jj:jj + 1, :]
        acc = term if acc is None else acc + term
    return acc


def _mix_kernel(*refs, s_t, l_t, pos0, has_hist):
    if has_hist:
        (x_ref, hist_ref, gmix_ref, win_ref, gv_ref, ws_ref, mb_ref, wpool_ref, spool_ref, wout_ref,
         x1_ref, v_ref, zt_ref, zext_ref) = refs
    else:
        (x_ref, gmix_ref, win_ref, gv_ref, ws_ref, mb_ref, wpool_ref, spool_ref, wout_ref,
         x1_ref, v_ref, zt_ref, zext_ref, carry_ref) = refs
    tm = s_t * l_t
    j = pl.program_id(1)

    x = x_ref[...].reshape(tm, D_MODEL)
    h = _rmsnorm(x, gmix_ref[...]).astype(BF16)
    proj = _dot(h, win_ref[...])
    ua = _gelu_tanh(proj[:, :2 * WIDTH_A])
    u = ua[:, :WIDTH_A]
    v = _rmsnorm(ua[:, WIDTH_A:], gv_ref[...])
    z = proj[:, 2 * WIDTH_A:]

    if has_hist:
        mixed = _gate_short(v.reshape(s_t, l_t, WIDTH_A), ws_ref, l_t) + mb_ref[...][None, :, :]
        a_out = u * mixed.reshape(tm, WIDTH_A)
    else:
        mixed = _gate_chunks(v, ws_ref, tm).reshape(tm // CHUNK, CHUNK, WIDTH_A) + mb_ref[...][None, :, :]
        a_out = u * mixed.reshape(tm, WIDTH_A)

    z3 = z.reshape(s_t, l_t, WIDTH_B)
    if has_hist:
        zext_ref[:, HALO - POOL_HIST:HALO, :] = hist_ref[...]
    else:
        @pl.when(j == 0)
        def _():
            zext_ref[:, 0:HALO, :] = jnp.zeros((s_t, HALO, WIDTH_B), F32)

        @pl.when(j > 0)
        def _():
            zext_ref[:, 0:HALO, :] = carry_ref[...]
    zext_ref[:, HALO:HALO + l_t, :] = z3
    if not has_hist:
        carry_ref[...] = z3[:, l_t - HALO:, :]
    pos = pos0 + j * l_t + lax.broadcasted_iota(jnp.int32, (1, l_t, 1), 1)
    pooled_out = []
    for g, w in enumerate(POOL_WINDOWS):
        sl = slice(g * POOL_GROUP_DIM, (g + 1) * POOL_GROUP_DIM)
        acc = zext_ref[:, HALO:HALO + l_t, sl]
        for k in range(1, w):
            acc = acc + zext_ref[:, HALO - k:HALO - k + l_t, sl]
        cnt = jnp.minimum(w, pos + 1).astype(F32)
        pooled = acc / cnt - z3[:, :, sl]
        pooled_out.append(_dot(pooled.reshape(tm, POOL_GROUP_DIM).astype(BF16), wpool_ref[g]))
    b_out = jnp.concatenate(pooled_out, axis=-1) * spool_ref[...]

    cat = jnp.concatenate([a_out, b_out], axis=-1).astype(BF16)
    x1 = x + _dot(cat, wout_ref[...])
    x1_ref[...] = x1.reshape(s_t, l_t, D_MODEL)
    zt_ref[...] = zext_ref[:, l_t + 1:l_t + HALO, :]
    if has_hist:
        v_ref[...] = v.reshape(s_t, l_t, WIDTH_A)
    else:
        v_ref[...] = v[tm - CHUNK:, :].reshape(1, CHUNK, WIDTH_A)


def _mix(x3, hist, layer, wts, *, s_t, l_t, pos0):
    s_all, l_all, _ = x3.shape
    has_hist = hist is not None
    grid = (s_all // s_t, l_all // l_t)
    x_spec = pl.BlockSpec((s_t, l_t, D_MODEL), lambda b, j: (b, j, 0))
    in_specs = [x_spec]
    args = [x3]
    if has_hist:
        in_specs.append(pl.BlockSpec((None, s_t, POOL_HIST, WIDTH_B), lambda b, j: (layer, b, 0, 0)))
        args.append(hist)
    gate_w = ("coef_s", "bias_s") if has_hist else ("w_s", "bias_p")
    for name in ("g_mix", "w_in", "g_v") + gate_w + ("w_pool", "s_pool", "w_out"):
        in_specs.append(_layer_spec(wts[name], layer))
        args.append(wts[name])
    if has_hist:
        v_shape, v_blk = (s_all, l_all, WIDTH_A), (s_t, l_t, WIDTH_A)
    else:
        v_shape, v_blk = (s_all, CHUNK, WIDTH_A), (1, CHUNK, WIDTH_A)
    out_shape = [jax.ShapeDtypeStruct(x3.shape, F32),
                 jax.ShapeDtypeStruct(v_shape, F32),
                 jax.ShapeDtypeStruct((s_all, POOL_HIST, WIDTH_B), F32)]
    out_specs = [x_spec,
                 pl.BlockSpec(v_blk, lambda b, j: (b, 0, 0)),
                 pl.BlockSpec((s_t, POOL_HIST, WIDTH_B), lambda b, j: (b, 0, 0))]
    scratch = [pltpu.VMEM((s_t, HALO + l_t, WIDTH_B), F32)]
    if not has_hist:
        scratch.append(pltpu.VMEM((s_t, HALO, WIDTH_B), F32))
    return pl.pallas_call(
        functools.partial(_mix_kernel, s_t=s_t, l_t=l_t, pos0=pos0, has_hist=has_hist),
        grid=grid, in_specs=in_specs, out_specs=out_specs, out_shape=out_shape,
        scratch_shapes=scratch, compiler_params=_PARAMS,
        name="mix_sample" if has_hist else "mix_prompt",
    )(*args)


def _softmax_rows(sc):
    m = jnp.max(sc, axis=-1, keepdims=True)
    e = jnp.exp(sc - m)
    return e * (1.0 / jnp.sum(e, axis=-1, keepdims=True))


def _attend_split(qs, k_ref, v_ref, s):
    heads = []
    for hh in range(X_HEADS):
        sl = slice(hh * X_HEAD_DIM, (hh + 1) * X_HEAD_DIM)
        p = _softmax_rows(_dot_nt(qs[:, sl].astype(BF16), k_ref[s, :, sl]))
        heads.append(_dot(p.astype(BF16), v_ref[s, :, sl]))
    return jnp.concatenate(heads, axis=-1)


def _attend_native(qs, k_ref, v_ref, s, l_t):
    q_rows = jnp.concatenate([qs[:, hh * X_HEAD_DIM:(hh + 1) * X_HEAD_DIM] for hh in range(X_HEADS)], axis=0)
    k2 = k_ref[0, s].reshape(N_MEM * X_HEADS, X_HEAD_DIM).astype(BF16)
    v2 = v_ref[0, s].reshape(N_MEM * X_HEADS, X_HEAD_DIM).astype(BF16)
    sc = _dot_nt(q_rows.astype(BF16), k2)
    r_i = lax.broadcasted_iota(jnp.int32, sc.shape, 0)
    c_i = lax.broadcasted_iota(jnp.int32, sc.shape, 1)
    same_head = (r_i // l_t) == (c_i % X_HEADS)
    p = _softmax_rows(jnp.where(same_head, sc, -jnp.inf))
    o_rows = _dot(p.astype(BF16), v2)
    return jnp.concatenate([o_rows[hh * l_t:(hh + 1) * l_t] for hh in range(X_HEADS)], axis=-1)


def _xattn_kernel(x_ref, k_ref, v_ref, g_ref, wq_ref, wo_ref, o_ref, *, s_t, l_t, native):
    tm = s_t * l_t
    x = x_ref[...].reshape(tm, D_MODEL)
    h = _rmsnorm(x, g_ref[...]).astype(BF16)
    q = _dot(h, wq_ref[...]) * (X_HEAD_DIM ** -0.5)
    outs = []
    for s in range(s_t):
        qs = q[s * l_t:(s + 1) * l_t]
        outs.append(_attend_native(qs, k_ref, v_ref, s, l_t) if native else _attend_split(qs, k_ref, v_ref, s))
    o = outs[0] if s_t == 1 else jnp.concatenate(outs, axis=0)
    x2 = x + _dot(o.astype(BF16), wo_ref[...])
    o_ref[...] = x2.reshape(s_t, l_t, D_MODEL)


def _xattn(x3, mem_k, mem_v, layer, wts, *, s_t, l_t):
    s_all, l_all, _ = x3.shape
    grid = (s_all // s_t, l_all // l_t)
    x_spec = pl.BlockSpec((s_t, l_t, D_MODEL), lambda b, j: (b, j, 0))
    native = mem_k.ndim == 5
    if native:
        m_spec = pl.BlockSpec((1, s_t, N_MEM, X_HEADS, X_HEAD_DIM), lambda b, j: (layer, b, 0, 0, 0))
    else:
        blk_off = layer * (s_all // s_t)
        m_spec = pl.BlockSpec((s_t, N_MEM, D_MODEL), lambda b, j: (blk_off + b, 0, 0))
    names = ("g_xattn", "w_xq", "w_xo")
    return pl.pallas_call(
        functools.partial(_xattn_kernel, s_t=s_t, l_t=l_t, native=native),
        grid=grid,
        in_specs=[x_spec, m_spec, m_spec] + [_layer_spec(wts[n], layer) for n in names],
        out_specs=x_spec,
        out_shape=jax.ShapeDtypeStruct(x3.shape, F32),
        compiler_params=_PARAMS,
        name="xattn_sample" if native else "xattn_prompt",
    )(x3, mem_k, mem_v, *[wts[n] for n in names])


def _first_index(vals, target):
    idx = jnp.full(target.shape, len(vals) - 1, jnp.int32)
    for i in range(len(vals) - 2, -1, -1):
        idx = jnp.where(vals[i] == target, i, idx)
    return idx


def _max_of(vals):
    m = vals[0]
    for v in vals[1:]:
        m = jnp.maximum(m, v)
    return m


def _route(lt):
    rows = [lt[i:i + 1, :] for i in range(N_GROUPS + N_EXPERTS)]
    g_l = rows[:N_GROUPS]
    gmax = _max_of(g_l)
    gsum = g_l[0] * 0.0
    for v in g_l:
        gsum = gsum + jnp.exp(v - gmax)
    g_p = 1.0 / gsum
    g_idx = _first_index(g_l, gmax)
    e_sel = []
    for j in range(PER_GROUP):
        v = rows[N_GROUPS + PER_GROUP * (N_GROUPS - 1) + j]
        for g in range(N_GROUPS - 2, -1, -1):
            v = jnp.where(g_idx == g, rows[N_GROUPS + PER_GROUP * g + j], v)
        e_sel.append(v)
    v1 = _max_of(e_sel)
    i1 = _first_index(e_sel, v1)
    neg = jnp.float32(-jnp.inf)
    e_rest = [jnp.where(i1 == j, neg, e_sel[j]) for j in range(PER_GROUP)]
    v2 = _max_of(e_rest)
    i2 = _first_index(e_rest, v2)
    t = jnp.exp(v2 - v1)
    w1 = g_p / (1.0 + t)
    w2 = g_p * t / (1.0 + t)
    gates = [jnp.where(i1 == j, w1, 0.0) + jnp.where(i2 == j, w2, 0.0) for j in range(PER_GROUP)]
    return g_idx, gates


def _stack_rows(rows, n_rows, tm):
    ri = lax.broadcasted_iota(jnp.int32, (n_rows, tm), 0)
    out = jnp.zeros((n_rows, tm), F32)
    for i, r in enumerate(rows):
        out = jnp.where(ri == i, r, out)
    return out


def _moe_kernel(x_ref, g_ref, wr_ref, br_ref, wg_ref, wu_ref, wd_ref, gf_ref, o_ref, ys_ref, *, s_t, l_t, final):
    tm = s_t * l_t
    x = x_ref[...].reshape(tm, D_MODEL)
    h = _rmsnorm(x, g_ref[...]).astype(BF16)
    logits = _dot(h, wr_ref[...]) + br_ref[...]
    g_idx, gates = _route(logits.T)

    member = [(g_idx == g).astype(F32) for g in range(N_GROUPS)]
    t_r = lax.broadcasted_iota(jnp.int32, (tm, tm), 0)
    t_c = lax.broadcasted_iota(jnp.int32, (tm, tm), 1)
    upper = jnp.where(t_r <= t_c, 1.0, 0.0).astype(BF16)
    incl = _dot(_stack_rows(member, GATE_ROWS, tm).astype(BF16), upper)
    end_blk = []
    posf = jnp.zeros((1, tm), F32)
    run = jnp.zeros((1, 1), F32)
    for g in range(N_GROUPS):
        n_g = jnp.sum(member[g], axis=-1, keepdims=True)
        posf = posf + member[g] * (run * MOE_BM + incl[g:g + 1, :] - 1.0)
        run = run + jnp.floor((n_g + (MOE_BM - 1)) * (1.0 / MOE_BM))
        end_blk.append(run)
    pos = posf.astype(jnp.int32)
    e0, e1, e2, n_blk = [e.astype(jnp.int32)[0, 0] for e in end_blk]

    g_hi = [gt.astype(BF16).astype(F32) for gt in gates]
    g_lo = [gt - hi for gt, hi in zip(gates, g_hi)]
    gmat = _stack_rows(g_hi + g_lo, GATE_ROWS, tm).astype(BF16)

    r_iota = lax.broadcasted_iota(jnp.int32, (MOE_BM, tm), 0)

    def block(b):
        r0 = pl.multiple_of(b * MOE_BM, MOE_BM)
        grp = (b >= e0).astype(jnp.int32) + (b >= e1).astype(jnp.int32) + (b >= e2).astype(jnp.int32)
        pb = jnp.where(r_iota + r0 == pos, 1.0, 0.0).astype(BF16)
        xb = _dot(pb, h).astype(BF16)
        gs = _dot_nt(pb, gmat)
        acc = jnp.zeros((MOE_BM, D_MODEL), F32)
        for j in range(PER_GROUP):
            e = grp * PER_GROUP + j
            hg = _dot(xb, wg_ref[e])
            hu = _dot(xb, wu_ref[e])
            gj = gs[:, j:j + 1] + gs[:, PER_GROUP + j:PER_GROUP + j + 1]
            a = (_silu(hg) * hu * gj).astype(BF16)
            acc = acc + _dot(a, wd_ref[e])
        ys_ref[pl.ds(r0, MOE_BM), :] = acc.astype(BF16)

    n_iter = lax.shift_right_logical(n_blk + (MOE_UNROLL - 1), MOE_UNROLL.bit_length() - 1)

    def blocks(i, carry):
        for u in range(MOE_UNROLL):
            block(i * MOE_UNROLL + u)
        return carry

    lax.fori_loop(0, n_iter, blocks, 0)

    def clear(b, carry):
        r0 = pl.multiple_of(b * MOE_BM, MOE_BM)
        ys_ref[pl.ds(r0, MOE_BM), :] = jnp.zeros((MOE_BM, D_MODEL), BF16)
        return carry

    lax.fori_loop(n_iter * MOE_UNROLL, MOE_R // MOE_BM, clear, 0)

    pos_col = jnp.broadcast_to(posf, (LANES, tm)).T
    lane = lax.broadcasted_iota(jnp.int32, (tm, LANES), 1).astype(F32)
    pt = jnp.concatenate(
        [jnp.where(pos_col == lane + float(LANES * k), 1.0, 0.0).astype(BF16) for k in range(MOE_R // LANES)],
        axis=-1)
    y = x + _dot(pt, ys_ref[...])
    if final:
        y = _rmsnorm(y, gf_ref[...])
    o_ref[...] = y.reshape(s_t, l_t, D_MODEL)


def _moe(x3, layer, wts, g_final, *, s_t, l_t, final):
    s_all, l_all, _ = x3.shape
    assert s_t * l_t == MOE_TM and MOE_R % (MOE_BM * MOE_UNROLL) == 0
    grid = (s_all // s_t, l_all // l_t)
    x_spec = pl.BlockSpec((s_t, l_t, D_MODEL), lambda b, j: (b, j, 0))
    names = ("g_ffn", "w_r", "b_r", "w_gate", "w_up", "w_down")
    return pl.pallas_call(
        functools.partial(_moe_kernel, s_t=s_t, l_t=l_t, final=final),
        grid=grid,
        in_specs=([x_spec] + [_layer_spec(wts[n], layer, pipeline_mode=pl.Buffered(1)) for n in names]
                  + [pl.BlockSpec(g_final.shape, lambda b, j: (0, 0))]),
        out_specs=x_spec,
        out_shape=jax.ShapeDtypeStruct(x3.shape, F32),
        scratch_shapes=[pltpu.VMEM((MOE_R, D_MODEL), BF16)],
        compiler_params=_PARAMS,
        name="moe",
    )(x3, *[wts[n] for n in names], g_final)


def kernel(x_prompt, x_sample, cache_mem_k, cache_mem_v, state_pool, mem_prompt, g_mix, w_in, g_v, w_s, b_s, w_pool, s_pool, w_out, g_mem, g_xattn, w_xq, w_xk, w_xv, w_xo, g_ffn, w_group, b_group, w_router, b_router, w_gate, w_up, w_down, g_final):
    bp, seq, _ = x_prompt.shape
    bs, dseq, _ = x_sample.shape

    mem_k, mem_v, mem_kb, mem_vb = _memkv(mem_prompt.reshape(bp * N_MEM, D_MODEL), g_mem.reshape(1, D_MODEL),
                                          w_xk.astype(BF16), w_xv.astype(BF16))
    mem_kb = mem_kb.reshape(DEPTH * bp, N_MEM, D_MODEL)
    mem_vb = mem_vb.reshape(DEPTH * bp, N_MEM, D_MODEL)

    pad_r = ROUTER_LANES - N_GROUPS - N_EXPERTS
    bias_full = jnp.repeat(jnp.swapaxes(b_s, 1, 2), HEAD_DIM_A, axis=2)
    wts = {
        "g_mix": g_mix.reshape(DEPTH, 1, D_MODEL), "w_in": w_in.astype(BF16), "g_v": g_v.reshape(DEPTH, 1, WIDTH_A),
        "w_s": w_s.astype(BF16), "bias_p": bias_full,
        "coef_s": jnp.repeat(jnp.transpose(w_s[:, :, :dseq, :dseq], (0, 3, 2, 1)), HEAD_DIM_A, axis=3),
        "bias_s": bias_full[:, :dseq, :],
        "w_pool": w_pool.astype(BF16), "s_pool": s_pool.reshape(DEPTH, 1, WIDTH_B), "w_out": w_out.astype(BF16),
        "g_xattn": g_xattn.reshape(DEPTH, 1, D_MODEL), "w_xq": w_xq.astype(BF16), "w_xo": w_xo.astype(BF16),
        "g_ffn": g_ffn.reshape(DEPTH, 1, D_MODEL),
        "w_r": jnp.concatenate([w_group, w_router, jnp.zeros((DEPTH, D_MODEL, pad_r), F32)], axis=-1).astype(BF16),
        "b_r": jnp.concatenate([b_group, b_router, jnp.zeros((DEPTH, pad_r), F32)], axis=-1).reshape(DEPTH, 1, ROUTER_LANES),
        "w_gate": w_gate.astype(BF16), "w_up": w_up.astype(BF16), "w_down": w_down.astype(BF16),
    }
    g_final2 = g_final.reshape(1, D_MODEL)

    yp, ys = x_prompt, x_sample
    cv_p, cv_s, pl_p, pl_s = [], [], [], []
    for l in range(DEPTH):
        final = l == DEPTH - 1

        yp, cvp, plp = _mix(yp, None, l, wts, s_t=1, l_t=PROMPT_TM, pos0=0)
        yp = _xattn(yp, mem_kb, mem_vb, l, wts, s_t=1, l_t=PROMPT_TM)
        yp = _moe(yp, l, wts, g_final2, s_t=1, l_t=MOE_TM, final=final)

        ys, cvs, pls = _mix(ys, state_pool, l, wts, s_t=SAMPLE_MIX_SEQS, l_t=dseq, pos0=PAST_LEN)
        ys = _xattn(ys, cache_mem_k, cache_mem_v, l, wts, s_t=SAMPLE_ATT_SEQS, l_t=dseq)
        ys = _moe(ys, l, wts, g_final2, s_t=MOE_TM // dseq, l_t=dseq, final=final)

        cv_p.append(cvp.reshape(bp, CHUNK, HEADS_A, HEAD_DIM_A))
        cv_s.append(cvs.reshape(bs, dseq, HEADS_A, HEAD_DIM_A))
        pl_p.append(plp)
        pl_s.append(pls)

    return (yp, ys, mem_k, mem_v, jnp.stack(cv_p), jnp.stack(cv_s), jnp.stack(pl_p), jnp.stack(pl_s))
```

```python
import functools

import jax
import jax.numpy as jnp
from jax import lax
from jax.experimental import pallas as pl
from jax.experimental.pallas import tpu as pltpu

D_MODEL = 1024
DEPTH = 4
PAST_LEN = 16384
CHUNK = 128
WIDTH_A = 512
WIDTH_B = 512
HEADS_A = 8
HEAD_DIM_A = 64
POOL_WINDOWS = (2, 4, 8, 16)
POOL_GROUP_DIM = 128
POOL_HIST = 15
HALO = 16
PROJ_WIDTH = 2 * WIDTH_A + WIDTH_B
N_MEM = 256
X_HEADS = 4
X_HEAD_DIM = 256
N_GROUPS = 4
PER_GROUP = 4
N_EXPERTS = 16
D_EXPERT = 256
EPS = 1e-6
ROUTER_LANES = 128
LANES = 128
GATE_ROWS = 16

PROMPT_TM = 512
MIX_SUB = 256
SAMPLE_MIX_SEQS = 32
SAMPLE_ATT_SEQS = 8
MOE_TM = 512
MOE_BM = 128
MOE_R = MOE_TM + N_GROUPS * MOE_BM
MOE_UNROLL = 2

F32 = jnp.float32
BF16 = jnp.bfloat16

VMEM_LIMIT_BYTES = 56 * 1024 * 1024


def _rmsnorm(x, g):
    r = lax.rsqrt(jnp.mean(x * x, axis=-1, keepdims=True) + EPS)
    return x * r * g


def _gelu_tanh(x):
    c = 0.7978845608028654
    return 0.5 * x * (1.0 + jnp.tanh(c * (x + 0.044715 * (x * x * x))))


def _silu(x):
    return x * (1.0 / (1.0 + jnp.exp(-x)))


def _dot(a, b):
    return jnp.dot(a, b, preferred_element_type=F32)


def _dot_nt(a, b):
    return lax.dot_general(a, b, (((1,), (1,)), ((), ())), preferred_element_type=F32)


def _layer_spec(arr, layer, **kw):
    nd = arr.ndim - 1
    return pl.BlockSpec((None,) + arr.shape[1:], lambda b, j: (layer,) + (0,) * nd, **kw)


_PARAMS = pltpu.CompilerParams(dimension_semantics=("arbitrary", "arbitrary"), vmem_limit_bytes=VMEM_LIMIT_BYTES)


def _memkv_kernel(mem_ref, g_ref, wk_ref, wv_ref, k_ref, v_ref, kb_ref, vb_ref):
    h = _rmsnorm(mem_ref[...], g_ref[...]).astype(BF16)
    k = _dot(h, wk_ref[0])
    v = _dot(h, wv_ref[0])
    for hh in range(X_HEADS):
        sl = slice(hh * X_HEAD_DIM, (hh + 1) * X_HEAD_DIM)
        k_ref[0, 0, :, hh, :] = k[:, sl]
        v_ref[0, 0, :, hh, :] = v[:, sl]
    kb_ref[0] = k.astype(BF16)
    vb_ref[0] = v.astype(BF16)


def _memkv(mem2d, g_mem, wk, wv):
    rows = mem2d.shape[0]
    out_sds = jax.ShapeDtypeStruct((DEPTH, rows // N_MEM, N_MEM, X_HEADS, X_HEAD_DIM), F32)
    outb_sds = jax.ShapeDtypeStruct((DEPTH, rows, D_MODEL), BF16)
    w_spec = pl.BlockSpec((1, D_MODEL, D_MODEL), lambda l, r: (l, 0, 0))
    o5_spec = pl.BlockSpec((1, 1, N_MEM, X_HEADS, X_HEAD_DIM), lambda l, r: (l, r, 0, 0, 0))
    o_spec = pl.BlockSpec((1, N_MEM, D_MODEL), lambda l, r: (l, r, 0))
    return pl.pallas_call(
        _memkv_kernel,
        grid=(DEPTH, rows // N_MEM),
        in_specs=[pl.BlockSpec((N_MEM, D_MODEL), lambda l, r: (r, 0)),
                  pl.BlockSpec((1, D_MODEL), lambda l, r: (0, 0)),
                  w_spec, w_spec],
        out_specs=[o5_spec, o5_spec, o_spec, o_spec],
        out_shape=[out_sds, out_sds, outb_sds, outb_sds],
        compiler_params=pltpu.CompilerParams(dimension_semantics=("arbitrary", "arbitrary")),
        name="memkv",
    )(mem2d, g_mem, wk, wv)


def _gate_chunks(v, ws_ref, tm):
    r_i = lax.broadcasted_iota(jnp.int32, (CHUNK, CHUNK), 0)
    c_i = lax.broadcasted_iota(jnp.int32, (CHUNK, CHUNK), 1)
    causal = c_i <= r_i
    lane = lax.broadcasted_iota(jnp.int32, (CHUNK, LANES), 1)
    zero_b = jnp.zeros((), BF16)
    vb = v.astype(BF16)
    pair_w = []
    for p in range(HEADS_A // 2):
        m_lo = jnp.where(causal, ws_ref[2 * p], zero_b)
        m_hi = jnp.where(causal, ws_ref[2 * p + 1], zero_b)
        pair_w.append(jnp.concatenate([m_lo, m_hi], axis=1))
    rows = []
    for c in range(tm // CHUNK):
        cols = []
        for p in range(HEADS_A // 2):
            vp = vb[c * CHUNK:(c + 1) * CHUNK, LANES * p:LANES * (p + 1)]
            lo = jnp.where(lane < HEAD_DIM_A, vp, zero_b)
            hi = jnp.where(lane >= HEAD_DIM_A, vp, zero_b)
            cols.append(_dot(pair_w[p], jnp.concatenate([lo, hi], axis=0)))
        rows.append(jnp.concatenate(cols, axis=-1))
    return jnp.concatenate(rows, axis=0)


def _gate_short(v3, coef_ref, l_t):
    i_i = lax.broadcasted_iota(jnp.int32, (l_t, WIDTH_A), 0)
    acc = None
    for jj in range(l_t):
        cj = jnp.where(i_i >= jj, coef_ref[jj], 0.0)
        term = cj[None, :, :] * v3[:, jj:jj + 1, :]
        acc = term if acc is None else acc + term
    return acc


def _trailing_sums(zx, n_new):
    outs = []
    for g, w in enumerate(POOL_WINDOWS):
        a = zx[:, :, g * POOL_GROUP_DIM:(g + 1) * POOL_GROUP_DIM]
        span = 1
        while span < w:
            n = a.shape[1]
            a = a[:, span:, :] + a[:, :n - span, :]
            span *= 2
        first = HALO + 1 - w
        outs.append(a[:, first:first + n_new, :])
    return outs


def _mix_kernel(*refs, s_t, l_t, pos0, has_hist):
    if has_hist:
        (x_ref, hist_ref, gmix_ref, win_ref, gv_ref, ws_ref, mb_ref, wpool_ref, spool_ref, wout_ref,
         x1_ref, v_ref, zt_ref, zext_ref) = refs
    else:
        (x_ref, gmix_ref, win_ref, gv_ref, ws_ref, mb_ref, wpool_ref, spool_ref, wout_ref,
         x1_ref, v_ref, zt_ref, zext_ref, carry_ref) = refs
    j = pl.program_id(1)
    sub = min(l_t, MIX_SUB)
    ts = s_t * sub

    if has_hist:
        zext_ref[:, HALO - POOL_HIST:HALO, :] = hist_ref[...]
    else:
        @pl.when(j == 0)
        def _():
            zext_ref[:, 0:HALO, :] = jnp.zeros((s_t, HALO, WIDTH_B), F32)

        @pl.when(j > 0)
        def _():
            zext_ref[:, 0:HALO, :] = carry_ref[...]

    for r0 in range(0, l_t, sub):
        x = x_ref[:, r0:r0 + sub, :].reshape(ts, D_MODEL)
        h = _rmsnorm(x, gmix_ref[...]).astype(BF16)
        proj = _dot(h, win_ref[...])
        ua = _gelu_tanh(proj[:, :2 * WIDTH_A])
        u = ua[:, :WIDTH_A]
        v = _rmsnorm(ua[:, WIDTH_A:], gv_ref[...])
        z3 = proj[:, 2 * WIDTH_A:].reshape(s_t, sub, WIDTH_B)

        if has_hist:
            mixed = _gate_short(v.reshape(s_t, sub, WIDTH_A), ws_ref, sub) + mb_ref[...][None, :, :]
        else:
            mixed = _gate_chunks(v, ws_ref, ts).reshape(ts // CHUNK, CHUNK, WIDTH_A) + mb_ref[...][None, :, :]
        a_out = u * mixed.reshape(ts, WIDTH_A)

        zext_ref[:, HALO + r0:HALO + r0 + sub, :] = z3
        sums = _trailing_sums(zext_ref[:, r0:r0 + HALO + sub, :], sub)
        pos = pos0 + j * l_t + r0 + lax.broadcasted_iota(jnp.int32, (1, sub, 1), 1)
        pooled_out = []
        for g, w in enumerate(POOL_WINDOWS):
            sl = slice(g * POOL_GROUP_DIM, (g + 1) * POOL_GROUP_DIM)
            inv_cnt = 1.0 / jnp.minimum(w, pos + 1).astype(F32)
            pooled = sums[g] * inv_cnt - z3[:, :, sl]
            pooled_out.append(_dot(pooled.reshape(ts, POOL_GROUP_DIM).astype(BF16), wpool_ref[g]))
        b_out = jnp.concatenate(pooled_out, axis=-1) * spool_ref[...]

        cat = jnp.concatenate([a_out, b_out], axis=-1).astype(BF16)
        x1 = x + _dot(cat, wout_ref[...])
        x1_ref[:, r0:r0 + sub, :] = x1.reshape(s_t, sub, D_MODEL)
        if has_hist:
            v_ref[...] = v.reshape(s_t, sub, WIDTH_A)
        elif r0 + sub == l_t:
            v_ref[...] = v[ts - CHUNK:, :].reshape(1, CHUNK, WIDTH_A)

    zt_ref[...] = zext_ref[:, l_t + 1:l_t + HALO, :]
    if not has_hist:
        carry_ref[...] = zext_ref[:, l_t:l_t + HALO, :]


def _mix(x3, hist, layer, wts, *, s_t, l_t, pos0):
    s_all, l_all, _ = x3.shape
    has_hist = hist is not None
    grid = (s_all // s_t, l_all // l_t)
    x_spec = pl.BlockSpec((s_t, l_t, D_MODEL), lambda b, j: (b, j, 0))
    in_specs = [x_spec]
    args = [x3]
    if has_hist:
        in_specs.append(pl.BlockSpec((None, s_t, POOL_HIST, WIDTH_B), lambda b, j: (layer, b, 0, 0)))
        args.append(hist)
    gate_w = ("coef_s", "bias_s") if has_hist else ("w_s", "bias_p")
    for name in ("g_mix", "w_in", "g_v") + gate_w + ("w_pool", "s_pool", "w_out"):
        in_specs.append(_layer_spec(wts[name], layer))
        args.append(wts[name])
    if has_hist:
        v_shape, v_blk = (s_all, l_all, WIDTH_A), (s_t, l_t, WIDTH_A)
    else:
        v_shape, v_blk = (s_all, CHUNK, WIDTH_A), (1, CHUNK, WIDTH_A)
    out_shape = [jax.ShapeDtypeStruct(x3.shape, F32),
                 jax.ShapeDtypeStruct(v_shape, F32),
                 jax.ShapeDtypeStruct((s_all, POOL_HIST, WIDTH_B), F32)]
    out_specs = [x_spec,
                 pl.BlockSpec(v_blk, lambda b, j: (b, 0, 0)),
                 pl.BlockSpec((s_t, POOL_HIST, WIDTH_B), lambda b, j: (b, 0, 0))]
    scratch = [pltpu.VMEM((s_t, HALO + l_t, WIDTH_B), F32)]
    if not has_hist:
        scratch.append(pltpu.VMEM((s_t, HALO, WIDTH_B), F32))
    return pl.pallas_call(
        functools.partial(_mix_kernel, s_t=s_t, l_t=l_t, pos0=pos0, has_hist=has_hist),
        grid=grid, in_specs=in_specs, out_specs=out_specs, out_shape=out_shape,
        scratch_shapes=scratch, compiler_params=_PARAMS,
        name="mix_sample" if has_hist else "mix_prompt",
    )(*args)


def _softmax_rows(sc):
    m = jnp.max(sc, axis=-1, keepdims=True)
    e = jnp.exp(sc - m)
    return e * (1.0 / jnp.sum(e, axis=-1, keepdims=True))


def _attend_split(qs, k_ref, v_ref, s):
    heads = []
    for hh in range(X_HEADS):
        sl = slice(hh * X_HEAD_DIM, (hh + 1) * X_HEAD_DIM)
        p = _softmax_rows(_dot_nt(qs[:, sl].astype(BF16), k_ref[s, :, sl]))
        heads.append(_dot(p.astype(BF16), v_ref[s, :, sl]))
    return jnp.concatenate(heads, axis=-1)


def _attend_native(qs, k_ref, v_ref, s, l_t):
    q_rows = jnp.concatenate([qs[:, hh * X_HEAD_DIM:(hh + 1) * X_HEAD_DIM] for hh in range(X_HEADS)], axis=0)
    k2 = k_ref[0, s].reshape(N_MEM * X_HEADS, X_HEAD_DIM).astype(BF16)
    v2 = v_ref[0, s].reshape(N_MEM * X_HEADS, X_HEAD_DIM).astype(BF16)
    sc = _dot_nt(q_rows.astype(BF16), k2)
    r_i = lax.broadcasted_iota(jnp.int32, sc.shape, 0)
    c_i = lax.broadcasted_iota(jnp.int32, sc.shape, 1)
    same_head = (r_i // l_t) == (c_i % X_HEADS)
    p = _softmax_rows(jnp.where(same_head, sc, -jnp.inf))
    o_rows = _dot(p.astype(BF16), v2)
    return jnp.concatenate([o_rows[hh * l_t:(hh + 1) * l_t] for hh in range(X_HEADS)], axis=-1)


def _xattn_kernel(xp_ref, xs_ref, kp_ref, vp_ref, ks_ref, vs_ref, g_ref, wq_ref, wo_ref, op_ref, os_ref,
                  *, l_p, s_s, l_s):
    n_s = s_s * l_s
    x = jnp.concatenate([xp_ref[...].reshape(l_p, D_MODEL), xs_ref[...].reshape(n_s, D_MODEL)], axis=0)
    h = _rmsnorm(x, g_ref[...]).astype(BF16)
    q = _dot(h, wq_ref[...]) * (X_HEAD_DIM ** -0.5)
    outs = [_attend_split(q[:l_p], kp_ref, vp_ref, 0)]
    for s in range(s_s):
        outs.append(_attend_native(q[l_p + s * l_s:l_p + (s + 1) * l_s], ks_ref, vs_ref, s, l_s))
    o = jnp.concatenate(outs, axis=0)
    x2 = x + _dot(o.astype(BF16), wo_ref[...])
    op_ref[...] = x2[:l_p].reshape(1, l_p, D_MODEL)
    os_ref[...] = x2[l_p:].reshape(s_s, l_s, D_MODEL)


def _xattn(xp, xs, mem_kb, mem_vb, cache_k, cache_v, layer, wts, *, l_p):
    bp, seq, _ = xp.shape
    s_all, l_s, _ = xs.shape
    tiles = seq // l_p
    s_s = s_all // (bp * tiles)
    assert s_s * bp * tiles == s_all and seq % l_p == 0
    xp_spec = pl.BlockSpec((1, l_p, D_MODEL), lambda b, j: (b, j, 0))
    xs_spec = pl.BlockSpec((s_s, l_s, D_MODEL), lambda b, j: (b * tiles + j, 0, 0))
    mp_spec = pl.BlockSpec((1, N_MEM, D_MODEL), lambda b, j: (layer * bp + b, 0, 0))
    ms_spec = pl.BlockSpec((1, s_s, N_MEM, X_HEADS, X_HEAD_DIM), lambda b, j: (layer, b * tiles + j, 0, 0, 0))
    names = ("g_xattn", "w_xq", "w_xo")
    return pl.pallas_call(
        functools.partial(_xattn_kernel, l_p=l_p, s_s=s_s, l_s=l_s),
        grid=(bp, tiles),
        in_specs=[xp_spec, xs_spec, mp_spec, mp_spec, ms_spec, ms_spec] + [_layer_spec(wts[n], layer) for n in names],
        out_specs=[xp_spec, xs_spec],
        out_shape=[jax.ShapeDtypeStruct(xp.shape, F32), jax.ShapeDtypeStruct(xs.shape, F32)],
        compiler_params=_PARAMS,
        name="xattn",
    )(xp, xs, mem_kb, mem_vb, cache_k, cache_v, *[wts[n] for n in names])


def _first_index(vals, target):
    idx = jnp.full(target.shape, len(vals) - 1, jnp.int32)
    for i in range(len(vals) - 2, -1, -1):
        idx = jnp.where(vals[i] == target, i, idx)
    return idx


def _max_of(vals):
    m = vals[0]
    for v in vals[1:]:
        m = jnp.maximum(m, v)
    return m


def _route(lt):
    rows = [lt[i:i + 1, :] for i in range(N_GROUPS + N_EXPERTS)]
    g_l = rows[:N_GROUPS]
    gmax = _max_of(g_l)
    gsum = g_l[0] * 0.0
    for v in g_l:
        gsum = gsum + jnp.exp(v - gmax)
    g_p = 1.0 / gsum
    g_idx = _first_index(g_l, gmax)
    e_sel = []
    for j in range(PER_GROUP):
        v = rows[N_GROUPS + PER_GROUP * (N_GROUPS - 1) + j]
        for g in range(N_GROUPS - 2, -1, -1):
            v = jnp.where(g_idx == g, rows[N_GROUPS + PER_GROUP * g + j], v)
        e_sel.append(v)
    v1 = _max_of(e_sel)
    i1 = _first_index(e_sel, v1)
    neg = jnp.float32(-jnp.inf)
    e_rest = [jnp.where(i1 == j, neg, e_sel[j]) for j in range(PER_GROUP)]
    v2 = _max_of(e_rest)
    i2 = _first_index(e_rest, v2)
    t = jnp.exp(v2 - v1)
    w1 = g_p / (1.0 + t)
    w2 = g_p * t / (1.0 + t)
    gates = [jnp.where(i1 == j, w1, 0.0) + jnp.where(i2 == j, w2, 0.0) for j in range(PER_GROUP)]
    return g_idx, gates


def _stack_rows(rows, n_rows, tm):
    ri = lax.broadcasted_iota(jnp.int32, (n_rows, tm), 0)
    out = jnp.zeros((n_rows, tm), F32)
    for i, r in enumerate(rows):
        out = jnp.where(ri == i, r, out)
    return out


def _moe_kernel(x_ref, g_ref, wr_ref, br_ref, wgu_ref, wd_ref, gf_ref, o_ref, ys_ref, *, s_t, l_t, final):
    tm = s_t * l_t
    x = x_ref[...].reshape(tm, D_MODEL)
    h = _rmsnorm(x, g_ref[...]).astype(BF16)
    logits = _dot(h, wr_ref[...]) + br_ref[...]
    g_idx, gates = _route(logits.T)

    member = [(g_idx == g).astype(F32) for g in range(N_GROUPS)]
    t_r = lax.broadcasted_iota(jnp.int32, (tm, tm), 0)
    t_c = lax.broadcasted_iota(jnp.int32, (tm, tm), 1)
    upper = jnp.where(t_r <= t_c, 1.0, 0.0).astype(BF16)
    incl = _dot(_stack_rows(member, GATE_ROWS, tm).astype(BF16), upper)
    end_blk = []
    posf = jnp.zeros((1, tm), F32)
    run = jnp.zeros((1, 1), F32)
    for g in range(N_GROUPS):
        n_g = jnp.sum(member[g], axis=-1, keepdims=True)
        posf = posf + member[g] * (run * MOE_BM + incl[g:g + 1, :] - 1.0)
        run = run + jnp.floor((n_g + (MOE_BM - 1)) * (1.0 / MOE_BM))
        end_blk.append(run)
    pos = posf.astype(jnp.int32)
    e0, e1, e2, n_blk = [e.astype(jnp.int32)[0, 0] for e in end_blk]

    g_hi = [gt.astype(BF16).astype(F32) for gt in gates]
    g_lo = [gt - hi for gt, hi in zip(gates, g_hi)]
    gmat = _stack_rows(g_hi + g_lo, GATE_ROWS, tm).astype(BF16)

    r_iota = lax.broadcasted_iota(jnp.int32, (MOE_BM, tm), 0)

    def block(b):
        r0 = pl.multiple_of(b * MOE_BM, MOE_BM)
        grp = (b >= e0).astype(jnp.int32) + (b >= e1).astype(jnp.int32) + (b >= e2).astype(jnp.int32)
        pb = jnp.where(r_iota + r0 == pos, 1.0, 0.0).astype(BF16)
        xb = _dot(pb, h).astype(BF16)
        gs = _dot_nt(pb, gmat)
        hgu = _dot(xb, wgu_ref[grp])
        width = PER_GROUP * D_EXPERT
        gate_cols = jnp.concatenate(
            [jnp.broadcast_to(gs[:, j:j + 1] + gs[:, PER_GROUP + j:PER_GROUP + j + 1], (MOE_BM, D_EXPERT))
             for j in range(PER_GROUP)], axis=-1)
        a = (_silu(hgu[:, :width]) * hgu[:, width:] * gate_cols).astype(BF16)
        ys_ref[pl.ds(r0, MOE_BM), :] = _dot(a, wd_ref[grp]).astype(BF16)

    n_iter = lax.shift_right_logical(n_blk + (MOE_UNROLL - 1), MOE_UNROLL.bit_length() - 1)

    def blocks(i, carry):
        for u in range(MOE_UNROLL):
            block(i * MOE_UNROLL + u)
        return carry

    lax.fori_loop(0, n_iter, blocks, 0)

    def clear(b, carry):
        r0 = pl.multiple_of(b * MOE_BM, MOE_BM)
        ys_ref[pl.ds(r0, MOE_BM), :] = jnp.zeros((MOE_BM, D_MODEL), BF16)
        return carry

    lax.fori_loop(n_iter * MOE_UNROLL, MOE_R // MOE_BM, clear, 0)

    pos_col = jnp.broadcast_to(posf, (LANES, tm)).T
    lane = lax.broadcasted_iota(jnp.int32, (tm, LANES), 1).astype(F32)
    pt = jnp.concatenate(
        [jnp.where(pos_col == lane + float(LANES * k), 1.0, 0.0).astype(BF16) for k in range(MOE_R // LANES)],
        axis=-1)
    y = x + _dot(pt, ys_ref[...])
    if final:
        y = _rmsnorm(y, gf_ref[...])
    o_ref[...] = y.reshape(s_t, l_t, D_MODEL)


def _moe(x3, layer, wts, g_final, *, s_t, l_t, final):
    s_all, l_all, _ = x3.shape
    assert s_t * l_t == MOE_TM and MOE_R % (MOE_BM * MOE_UNROLL) == 0
    grid = (s_all // s_t, l_all // l_t)
    x_spec = pl.BlockSpec((s_t, l_t, D_MODEL), lambda b, j: (b, j, 0))
    names = ("g_ffn", "w_r", "b_r", "w_gu", "w_dn")
    return pl.pallas_call(
        functools.partial(_moe_kernel, s_t=s_t, l_t=l_t, final=final),
        grid=grid,
        in_specs=([x_spec] + [_layer_spec(wts[n], layer, pipeline_mode=pl.Buffered(1)) for n in names]
                  + [pl.BlockSpec(g_final.shape, lambda b, j: (0, 0))]),
        out_specs=x_spec,
        out_shape=jax.ShapeDtypeStruct(x3.shape, F32),
        scratch_shapes=[pltpu.VMEM((MOE_R, D_MODEL), BF16)],
        compiler_params=_PARAMS,
        name="moe",
    )(x3, *[wts[n] for n in names], g_final)


def _group_columns(w):
    w = w.reshape(DEPTH, N_GROUPS, PER_GROUP, D_MODEL, D_EXPERT)
    return jnp.transpose(w, (0, 1, 3, 2, 4)).reshape(DEPTH, N_GROUPS, D_MODEL, PER_GROUP * D_EXPERT)


def kernel(x_prompt, x_sample, cache_mem_k, cache_mem_v, state_pool, mem_prompt, g_mix, w_in, g_v, w_s, b_s, w_pool, s_pool, w_out, g_mem, g_xattn, w_xq, w_xk, w_xv, w_xo, g_ffn, w_group, b_group, w_router, b_router, w_gate, w_up, w_down, g_final):
    bp, seq, _ = x_prompt.shape
    bs, dseq, _ = x_sample.shape

    mem_k, mem_v, mem_kb, mem_vb = _memkv(mem_prompt.reshape(bp * N_MEM, D_MODEL), g_mem.reshape(1, D_MODEL),
                                          w_xk.astype(BF16), w_xv.astype(BF16))
    mem_kb = mem_kb.reshape(DEPTH * bp, N_MEM, D_MODEL)
    mem_vb = mem_vb.reshape(DEPTH * bp, N_MEM, D_MODEL)

    pad_r = ROUTER_LANES - N_GROUPS - N_EXPERTS
    bias_full = jnp.repeat(jnp.swapaxes(b_s, 1, 2), HEAD_DIM_A, axis=2)
    wts = {
        "g_mix": g_mix.reshape(DEPTH, 1, D_MODEL), "w_in": w_in.astype(BF16), "g_v": g_v.reshape(DEPTH, 1, WIDTH_A),
        "w_s": w_s.astype(BF16), "bias_p": bias_full,
        "coef_s": jnp.repeat(jnp.transpose(w_s[:, :, :dseq, :dseq], (0, 3, 2, 1)), HEAD_DIM_A, axis=3),
        "bias_s": bias_full[:, :dseq, :],
        "w_pool": w_pool.astype(BF16), "s_pool": s_pool.reshape(DEPTH, 1, WIDTH_B), "w_out": w_out.astype(BF16),
        "g_xattn": g_xattn.reshape(DEPTH, 1, D_MODEL), "w_xq": w_xq.astype(BF16), "w_xo": w_xo.astype(BF16),
        "g_ffn": g_ffn.reshape(DEPTH, 1, D_MODEL),
        "w_r": jnp.concatenate([w_group, w_router, jnp.zeros((DEPTH, D_MODEL, pad_r), F32)], axis=-1).astype(BF16),
        "b_r": jnp.concatenate([b_group, b_router, jnp.zeros((DEPTH, pad_r), F32)], axis=-1).reshape(DEPTH, 1, ROUTER_LANES),
        "w_gu": jnp.concatenate([_group_columns(w_gate), _group_columns(w_up)], axis=-1).astype(BF16),
        "w_dn": w_down.reshape(DEPTH, N_GROUPS, PER_GROUP * D_EXPERT, D_MODEL).astype(BF16),
    }
    g_final2 = g_final.reshape(1, D_MODEL)

    yp, ys = x_prompt, x_sample
    cv_p, cv_s, pl_p, pl_s = [], [], [], []
    for l in range(DEPTH):
        final = l == DEPTH - 1

        yp, cvp, plp = _mix(yp, None, l, wts, s_t=1, l_t=PROMPT_TM, pos0=0)
        ys, cvs, pls = _mix(ys, state_pool, l, wts, s_t=SAMPLE_MIX_SEQS, l_t=dseq, pos0=PAST_LEN)
        yp, ys = _xattn(yp, ys, mem_kb, mem_vb, cache_mem_k, cache_mem_v, l, wts, l_p=PROMPT_TM)
        yp = _moe(yp, l, wts, g_final2, s_t=1, l_t=MOE_TM, final=final)
        ys = _moe(ys, l, wts, g_final2, s_t=MOE_TM // dseq, l_t=dseq, final=final)

        cv_p.append(cvp.reshape(bp, CHUNK, HEADS_A, HEAD_DIM_A))
        cv_s.append(cvs.reshape(bs, dseq, HEADS_A, HEAD_DIM_A))
        pl_p.append(plp)
        pl_s.append(pls)

    return (yp, ys, mem_k, mem_v, jnp.stack(cv_p), jnp.stack(cv_s), jnp.stack(pl_p), jnp.stack(pl_s))
```

```python
import functools

import jax
import jax.numpy as jnp
from jax import lax
from jax.experimental import pallas as pl
from jax.experimental.pallas import tpu as pltpu

D_MODEL = 1024
DEPTH = 4
PAST_LEN = 16384
CHUNK = 128
WIDTH_A = 512
WIDTH_B = 512
HEADS_A = 8
HEAD_DIM_A = 64
POOL_WINDOWS = (2, 4, 8, 16)
POOL_GROUP_DIM = 128
POOL_HIST = 15
HALO = 16
PROJ_WIDTH = 2 * WIDTH_A + WIDTH_B
N_MEM = 256
X_HEADS = 4
X_HEAD_DIM = 256
N_GROUPS = 4
PER_GROUP = 4
N_EXPERTS = 16
D_EXPERT = 256
EPS = 1e-6
ROUTER_LANES = 128
LANES = 128
GATE_ROWS = 16

PROMPT_TM = 512
MIX_SUB = 256
SAMPLE_MIX_SEQS = 32
SAMPLE_ATT_SEQS = 8
MOE_TM = 512
MOE_BM = 128
MOE_R = MOE_TM + N_GROUPS * MOE_BM
MOE_UNROLL = 2

F32 = jnp.float32
BF16 = jnp.bfloat16

VMEM_LIMIT_BYTES = 56 * 1024 * 1024


def _rmsnorm(x, g):
    r = lax.rsqrt(jnp.mean(x * x, axis=-1, keepdims=True) + EPS)
    return x * r * g


def _gelu_tanh(x):
    c = 0.7978845608028654
    return 0.5 * x * (1.0 + jnp.tanh(c * (x + 0.044715 * (x * x * x))))


def _silu(x):
    return x * (1.0 / (1.0 + jnp.exp(-x)))


def _dot(a, b):
    return jnp.dot(a, b, preferred_element_type=F32)


def _dot_nt(a, b):
    return lax.dot_general(a, b, (((1,), (1,)), ((), ())), preferred_element_type=F32)


def _layer_spec(arr, layer, **kw):
    nd = arr.ndim - 1
    return pl.BlockSpec((None,) + arr.shape[1:], lambda b, j: (layer,) + (0,) * nd, **kw)


_PARAMS = pltpu.CompilerParams(dimension_semantics=("arbitrary", "arbitrary"), vmem_limit_bytes=VMEM_LIMIT_BYTES)


def _memkv_kernel(mem_ref, g_ref, wk_ref, wv_ref, win_ref, wout_ref, wq_ref, wo_ref,
                  k_ref, v_ref, kb_ref, vb_ref, win_o, wout_o, wq_o, wo_o, wkb_ref, wvb_ref):
    win_o[...] = win_ref[...].astype(BF16)
    wout_o[...] = wout_ref[...].astype(BF16)
    wq_o[...] = wq_ref[...].astype(BF16)
    wo_o[...] = wo_ref[...].astype(BF16)

    @pl.when(pl.program_id(1) == 0)
    def _():
        wkb_ref[...] = wk_ref[0].astype(BF16)
        wvb_ref[...] = wv_ref[0].astype(BF16)

    h = _rmsnorm(mem_ref[...], g_ref[...]).astype(BF16)
    k = _dot(h, wkb_ref[...])
    v = _dot(h, wvb_ref[...])
    for hh in range(X_HEADS):
        sl = slice(hh * X_HEAD_DIM, (hh + 1) * X_HEAD_DIM)
        k_ref[0, 0, :, hh, :] = k[:, sl]
        v_ref[0, 0, :, hh, :] = v[:, sl]
    kb_ref[0] = k.astype(BF16)
    vb_ref[0] = v.astype(BF16)


def _memkv(mem2d, g_mem, wk, wv, w_in, w_out, w_xq, w_xo):
    rows = mem2d.shape[0]
    n_r = rows // N_MEM
    slab = D_MODEL // n_r
    out_sds = jax.ShapeDtypeStruct((DEPTH, n_r, N_MEM, X_HEADS, X_HEAD_DIM), F32)
    outb_sds = jax.ShapeDtypeStruct((DEPTH, rows, D_MODEL), BF16)
    w_spec = pl.BlockSpec((1, D_MODEL, D_MODEL), lambda l, r: (l, 0, 0))
    o5_spec = pl.BlockSpec((1, 1, N_MEM, X_HEADS, X_HEAD_DIM), lambda l, r: (l, r, 0, 0, 0))
    o_spec = pl.BlockSpec((1, N_MEM, D_MODEL), lambda l, r: (l, r, 0))
    dense = (w_in, w_out, w_xq, w_xo)
    slab_specs = [pl.BlockSpec((1, slab, w.shape[2]), lambda l, r: (l, r, 0)) for w in dense]
    return pl.pallas_call(
        _memkv_kernel,
        grid=(DEPTH, n_r),
        in_specs=[pl.BlockSpec((N_MEM, D_MODEL), lambda l, r: (r, 0)),
                  pl.BlockSpec((1, D_MODEL), lambda l, r: (0, 0)),
                  w_spec, w_spec] + slab_specs,
        out_specs=[o5_spec, o5_spec, o_spec, o_spec] + slab_specs,
        out_shape=[out_sds, out_sds, outb_sds, outb_sds] + [jax.ShapeDtypeStruct(w.shape, BF16) for w in dense],
        scratch_shapes=[pltpu.VMEM((D_MODEL, D_MODEL), BF16), pltpu.VMEM((D_MODEL, D_MODEL), BF16)],
        compiler_params=_PARAMS,
        name="memkv",
    )(mem2d, g_mem, wk, wv, *dense)


def _gate_chunks(v, ws_ref, tm):
    r_i = lax.broadcasted_iota(jnp.int32, (CHUNK, CHUNK), 0)
    c_i = lax.broadcasted_iota(jnp.int32, (CHUNK, CHUNK), 1)
    causal = c_i <= r_i
    lane = lax.broadcasted_iota(jnp.int32, (CHUNK, LANES), 1)
    zero_b = jnp.zeros((), BF16)
    vb = v.astype(BF16)
    pair_w = []
    for p in range(HEADS_A // 2):
        m_lo = jnp.where(causal, ws_ref[2 * p], zero_b)
        m_hi = jnp.where(causal, ws_ref[2 * p + 1], zero_b)
        pair_w.append(jnp.concatenate([m_lo, m_hi], axis=1))
    rows = []
    for c in range(tm // CHUNK):
        cols = []
        for p in range(HEADS_A // 2):
            vp = vb[c * CHUNK:(c + 1) * CHUNK, LANES * p:LANES * (p + 1)]
            lo = jnp.where(lane < HEAD_DIM_A, vp, zero_b)
            hi = jnp.where(lane >= HEAD_DIM_A, vp, zero_b)
            cols.append(_dot(pair_w[p], jnp.concatenate([lo, hi], axis=0)))
        rows.append(jnp.concatenate(cols, axis=-1))
    return jnp.concatenate(rows, axis=0)


def _gate_short(v3, coef_ref, l_t):
    i_i = lax.broadcasted_iota(jnp.int32, (l_t, WIDTH_A), 0)
    acc = None
    for jj in range(l_t):
        cj = jnp.where(i_i >= jj, coef_ref[jj], 0.0)
        term = cj[None, :, :] * v3[:, jj:jj + 1, :]
        acc = term if acc is None else acc + term
    return acc


def _trailing_sums(zx, n_new):
    outs = []
    for g, w in enumerate(POOL_WINDOWS):
        a = zx[:, :, g * POOL_GROUP_DIM:(g + 1) * POOL_GROUP_DIM]
        span = 1
        while span < w:
            n = a.shape[1]
            a = a[:, span:, :] + a[:, :n - span, :]
            span *= 2
        first = HALO + 1 - w
        outs.append(a[:, first:first + n_new, :])
    return outs


def _mix_kernel(*refs, s_t, l_t, pos0, has_hist):
    if has_hist:
        (x_ref, hist_ref, gmix_ref, win_ref, gv_ref, ws_ref, mb_ref, wpool_ref, spool_ref, wout_ref,
         x1_ref, v_ref, zt_ref, zext_ref) = refs
    else:
        (x_ref, gmix_ref, win_ref, gv_ref, ws_ref, mb_ref, wpool_ref, spool_ref, wout_ref,
         x1_ref, v_ref, zt_ref, zext_ref, carry_ref) = refs
    j = pl.program_id(1)
    sub = min(l_t, MIX_SUB)
    ts = s_t * sub

    if has_hist:
        zext_ref[:, HALO - POOL_HIST:HALO, :] = hist_ref[...]
    else:
        @pl.when(j == 0)
        def _():
            zext_ref[:, 0:HALO, :] = jnp.zeros((s_t, HALO, WIDTH_B), F32)

        @pl.when(j > 0)
        def _():
            zext_ref[:, 0:HALO, :] = carry_ref[...]

    for r0 in range(0, l_t, sub):
        x = x_ref[:, r0:r0 + sub, :].reshape(ts, D_MODEL)
        h = _rmsnorm(x, gmix_ref[...]).astype(BF16)
        proj = _dot(h, win_ref[...])
        ua = _gelu_tanh(proj[:, :2 * WIDTH_A])
        u = ua[:, :WIDTH_A]
        v = _rmsnorm(ua[:, WIDTH_A:], gv_ref[...])
        z3 = proj[:, 2 * WIDTH_A:].reshape(s_t, sub, WIDTH_B)

        if has_hist:
            mixed = _gate_short(v.reshape(s_t, sub, WIDTH_A), ws_ref, sub) + mb_ref[...][None, :, :]
        else:
            mixed = _gate_chunks(v, ws_ref, ts).reshape(ts // CHUNK, CHUNK, WIDTH_A) + mb_ref[...][None, :, :]
        a_out = u * mixed.reshape(ts, WIDTH_A)

        zext_ref[:, HALO + r0:HALO + r0 + sub, :] = z3
        sums = _trailing_sums(zext_ref[:, r0:r0 + HALO + sub, :], sub)
        pos = pos0 + j * l_t + r0 + lax.broadcasted_iota(jnp.int32, (1, sub, 1), 1)
        pooled_out = []
        for g, w in enumerate(POOL_WINDOWS):
            sl = slice(g * POOL_GROUP_DIM, (g + 1) * POOL_GROUP_DIM)
            inv_cnt = 1.0 / jnp.minimum(w, pos + 1).astype(F32)
            pooled = sums[g] * inv_cnt - z3[:, :, sl]
            pooled_out.append(_dot(pooled.reshape(ts, POOL_GROUP_DIM).astype(BF16), wpool_ref[g]))
        b_out = jnp.concatenate(pooled_out, axis=-1) * spool_ref[...]

        cat = jnp.concatenate([a_out, b_out], axis=-1).astype(BF16)
        x1 = x + _dot(cat, wout_ref[...])
        x1_ref[:, r0:r0 + sub, :] = x1.reshape(s_t, sub, D_MODEL)
        if has_hist:
            v_ref[...] = v.reshape(s_t, sub, WIDTH_A)
        elif r0 + sub == l_t:
            v_ref[...] = v[ts - CHUNK:, :].reshape(1, CHUNK, WIDTH_A)

    zt_ref[...] = zext_ref[:, l_t + 1:l_t + HALO, :]
    if not has_hist:
        carry_ref[...] = zext_ref[:, l_t:l_t + HALO, :]


def _mix(x3, hist, layer, wts, *, s_t, l_t, pos0):
    s_all, l_all, _ = x3.shape
    has_hist = hist is not None
    grid = (s_all // s_t, l_all // l_t)
    x_spec = pl.BlockSpec((s_t, l_t, D_MODEL), lambda b, j: (b, j, 0))
    in_specs = [x_spec]
    args = [x3]
    if has_hist:
        in_specs.append(pl.BlockSpec((None, s_t, POOL_HIST, WIDTH_B), lambda b, j: (layer, b, 0, 0)))
        args.append(hist)
    gate_w = ("coef_s", "bias_s") if has_hist else ("w_s", "bias_p")
    for name in ("g_mix", "w_in", "g_v") + gate_w + ("w_pool", "s_pool", "w_out"):
        in_specs.append(_layer_spec(wts[name], layer))
        args.append(wts[name])
    if has_hist:
        v_shape, v_blk = (s_all, l_all, WIDTH_A), (s_t, l_t, WIDTH_A)
    else:
        v_shape, v_blk = (s_all, CHUNK, WIDTH_A), (1, CHUNK, WIDTH_A)
    out_shape = [jax.ShapeDtypeStruct(x3.shape, F32),
                 jax.ShapeDtypeStruct(v_shape, F32),
                 jax.ShapeDtypeStruct((s_all, POOL_HIST, WIDTH_B), F32)]
    out_specs = [x_spec,
                 pl.BlockSpec(v_blk, lambda b, j: (b, 0, 0)),
                 pl.BlockSpec((s_t, POOL_HIST, WIDTH_B), lambda b, j: (b, 0, 0))]
    scratch = [pltpu.VMEM((s_t, HALO + l_t, WIDTH_B), F32)]
    if not has_hist:
        scratch.append(pltpu.VMEM((s_t, HALO, WIDTH_B), F32))
    return pl.pallas_call(
        functools.partial(_mix_kernel, s_t=s_t, l_t=l_t, pos0=pos0, has_hist=has_hist),
        grid=grid, in_specs=in_specs, out_specs=out_specs, out_shape=out_shape,
        scratch_shapes=scratch, compiler_params=_PARAMS,
        name="mix_sample" if has_hist else "mix_prompt",
    )(*args)


def _softmax_rows(sc):
    m = jnp.max(sc, axis=-1, keepdims=True)
    e = jnp.exp(sc - m)
    return e * (1.0 / jnp.sum(e, axis=-1, keepdims=True))


def _attend_split(qs, k_ref, v_ref, s):
    heads = []
    for hh in range(X_HEADS):
        sl = slice(hh * X_HEAD_DIM, (hh + 1) * X_HEAD_DIM)
        p = _softmax_rows(_dot_nt(qs[:, sl].astype(BF16), k_ref[s, :, sl]))
        heads.append(_dot(p.astype(BF16), v_ref[s, :, sl]))
    return jnp.concatenate(heads, axis=-1)


def _attend_native(qs, k_ref, v_ref, s, l_t):
    q_rows = jnp.concatenate([qs[:, hh * X_HEAD_DIM:(hh + 1) * X_HEAD_DIM] for hh in range(X_HEADS)], axis=0)
    k2 = k_ref[0, s].reshape(N_MEM * X_HEADS, X_HEAD_DIM).astype(BF16)
    v2 = v_ref[0, s].reshape(N_MEM * X_HEADS, X_HEAD_DIM).astype(BF16)
    sc = _dot_nt(q_rows.astype(BF16), k2)
    r_i = lax.broadcasted_iota(jnp.int32, sc.shape, 0)
    c_i = lax.broadcasted_iota(jnp.int32, sc.shape, 1)
    same_head = (r_i // l_t) == (c_i % X_HEADS)
    p = _softmax_rows(jnp.where(same_head, sc, -jnp.inf))
    o_rows = _dot(p.astype(BF16), v2)
    return jnp.concatenate([o_rows[hh * l_t:(hh + 1) * l_t] for hh in range(X_HEADS)], axis=-1)


def _xattn_kernel(xp_ref, xs_ref, kp_ref, vp_ref, ks_ref, vs_ref, g_ref, wq_ref, wo_ref, eg_ref, eu_ref, ed_ref,
                  op_ref, os_ref, eg_o, eu_o, ed_o, *, l_p, s_s, l_s):
    eg_o[...] = eg_ref[...].astype(BF16)
    eu_o[...] = eu_ref[...].astype(BF16)
    ed_o[...] = ed_ref[...].astype(BF16)
    n_s = s_s * l_s
    x = jnp.concatenate([xp_ref[...].reshape(l_p, D_MODEL), xs_ref[...].reshape(n_s, D_MODEL)], axis=0)
    h = _rmsnorm(x, g_ref[...]).astype(BF16)
    q = _dot(h, wq_ref[...]) * (X_HEAD_DIM ** -0.5)
    outs = [_attend_split(q[:l_p], kp_ref, vp_ref, 0)]
    for s in range(s_s):
        outs.append(_attend_native(q[l_p + s * l_s:l_p + (s + 1) * l_s], ks_ref, vs_ref, s, l_s))
    o = jnp.concatenate(outs, axis=0)
    x2 = x + _dot(o.astype(BF16), wo_ref[...])
    op_ref[...] = x2[:l_p].reshape(1, l_p, D_MODEL)
    os_ref[...] = x2[l_p:].reshape(s_s, l_s, D_MODEL)


def _xattn(xp, xs, mem_kb, mem_vb, cache_k, cache_v, layer, wts, *, l_p):
    bp, seq, _ = xp.shape
    s_all, l_s, _ = xs.shape
    tiles = seq // l_p
    steps = bp * tiles
    s_s = s_all // steps
    parts = steps // N_EXPERTS
    assert s_s * steps == s_all and seq % l_p == 0 and parts * N_EXPERTS == steps
    rows_in, rows_mid = D_MODEL // parts, D_EXPERT // parts

    def step(b, j):
        return b * tiles + j

    xp_spec = pl.BlockSpec((1, l_p, D_MODEL), lambda b, j: (b, j, 0))
    xs_spec = pl.BlockSpec((s_s, l_s, D_MODEL), lambda b, j: (step(b, j), 0, 0))
    mp_spec = pl.BlockSpec((1, N_MEM, D_MODEL), lambda b, j: (layer * bp + b, 0, 0))
    ms_spec = pl.BlockSpec((1, s_s, N_MEM, X_HEADS, X_HEAD_DIM), lambda b, j: (layer, step(b, j), 0, 0, 0))
    gu_in = pl.BlockSpec((None, None, rows_in, D_EXPERT),
                         lambda b, j: (layer, step(b, j) // parts, step(b, j) % parts, 0))
    dn_in = pl.BlockSpec((None, None, rows_mid, D_MODEL),
                         lambda b, j: (layer, step(b, j) // parts, step(b, j) % parts, 0))
    gu_out = pl.BlockSpec((None, rows_in, D_EXPERT),
                          lambda b, j: (step(b, j) // parts // PER_GROUP, step(b, j) % parts,
                                        step(b, j) // parts % PER_GROUP))
    dn_out = pl.BlockSpec((None, rows_mid, D_MODEL),
                          lambda b, j: (step(b, j) // parts // PER_GROUP,
                                        (step(b, j) // parts % PER_GROUP) * parts + step(b, j) % parts, 0))
    names = ("g_xattn", "w_xq", "w_xo")
    gu_sds = jax.ShapeDtypeStruct((N_GROUPS, D_MODEL, PER_GROUP * D_EXPERT), BF16)
    dn_sds = jax.ShapeDtypeStruct((N_GROUPS, PER_GROUP * D_EXPERT, D_MODEL), BF16)
    return pl.pallas_call(
        functools.partial(_xattn_kernel, l_p=l_p, s_s=s_s, l_s=l_s),
        grid=(bp, tiles),
        in_specs=([xp_spec, xs_spec, mp_spec, mp_spec, ms_spec, ms_spec] + [_layer_spec(wts[n], layer) for n in names]
                  + [gu_in, gu_in, dn_in]),
        out_specs=[xp_spec, xs_spec, gu_out, gu_out, dn_out],
        out_shape=[jax.ShapeDtypeStruct(xp.shape, F32), jax.ShapeDtypeStruct(xs.shape, F32), gu_sds, gu_sds, dn_sds],
        compiler_params=_PARAMS,
        name="xattn",
    )(xp, xs, mem_kb, mem_vb, cache_k, cache_v, *[wts[n] for n in names],
      wts["w_gate"], wts["w_up"], wts["w_down"])


def _first_index(vals, target):
    idx = jnp.full(target.shape, len(vals) - 1, jnp.int32)
    for i in range(len(vals) - 2, -1, -1):
        idx = jnp.where(vals[i] == target, i, idx)
    return idx


def _max_of(vals):
    m = vals[0]
    for v in vals[1:]:
        m = jnp.maximum(m, v)
    return m


def _route(lt):
    rows = [lt[i:i + 1, :] for i in range(N_GROUPS + N_EXPERTS)]
    g_l = rows[:N_GROUPS]
    gmax = _max_of(g_l)
    gsum = g_l[0] * 0.0
    for v in g_l:
        gsum = gsum + jnp.exp(v - gmax)
    g_p = 1.0 / gsum
    g_idx = _first_index(g_l, gmax)
    e_sel = []
    for j in range(PER_GROUP):
        v = rows[N_GROUPS + PER_GROUP * (N_GROUPS - 1) + j]
        for g in range(N_GROUPS - 2, -1, -1):
            v = jnp.where(g_idx == g, rows[N_GROUPS + PER_GROUP * g + j], v)
        e_sel.append(v)
    v1 = _max_of(e_sel)
    i1 = _first_index(e_sel, v1)
    neg = jnp.float32(-jnp.inf)
    e_rest = [jnp.where(i1 == j, neg, e_sel[j]) for j in range(PER_GROUP)]
    v2 = _max_of(e_rest)
    i2 = _first_index(e_rest, v2)
    t = jnp.exp(v2 - v1)
    w1 = g_p / (1.0 + t)
    w2 = g_p * t / (1.0 + t)
    gates = [jnp.where(i1 == j, w1, 0.0) + jnp.where(i2 == j, w2, 0.0) for j in range(PER_GROUP)]
    return g_idx, gates


def _stack_rows(rows, n_rows, tm):
    ri = lax.broadcasted_iota(jnp.int32, (n_rows, tm), 0)
    out = jnp.zeros((n_rows, tm), F32)
    for i, r in enumerate(rows):
        out = jnp.where(ri == i, r, out)
    return out


def _moe_kernel(x_ref, g_ref, wr_ref, br_ref, wg_ref, wu_ref, wd_ref, gf_ref, o_ref, ys_ref, *, s_t, l_t, final):
    tm = s_t * l_t
    x = x_ref[...].reshape(tm, D_MODEL)
    h = _rmsnorm(x, g_ref[...]).astype(BF16)
    logits = _dot(h, wr_ref[...]) + br_ref[...]
    g_idx, gates = _route(logits.T)

    member = [(g_idx == g).astype(F32) for g in range(N_GROUPS)]
    t_r = lax.broadcasted_iota(jnp.int32, (tm, tm), 0)
    t_c = lax.broadcasted_iota(jnp.int32, (tm, tm), 1)
    upper = jnp.where(t_r <= t_c, 1.0, 0.0).astype(BF16)
    incl = _dot(_stack_rows(member, GATE_ROWS, tm).astype(BF16), upper)
    end_blk = []
    posf = jnp.zeros((1, tm), F32)
    run = jnp.zeros((1, 1), F32)
    for g in range(N_GROUPS):
        n_g = jnp.sum(member[g], axis=-1, keepdims=True)
        posf = posf + member[g] * (run * MOE_BM + incl[g:g + 1, :] - 1.0)
        run = run + jnp.floor((n_g + (MOE_BM - 1)) * (1.0 / MOE_BM))
        end_blk.append(run)
    pos = posf.astype(jnp.int32)
    e0, e1, e2, n_blk = [e.astype(jnp.int32)[0, 0] for e in end_blk]

    g_hi = [gt.astype(BF16).astype(F32) for gt in gates]
    g_lo = [gt - hi for gt, hi in zip(gates, g_hi)]
    gmat = _stack_rows(g_hi + g_lo, GATE_ROWS, tm).astype(BF16)

    r_iota = lax.broadcasted_iota(jnp.int32, (MOE_BM, tm), 0)

    def block(b):
        r0 = pl.multiple_of(b * MOE_BM, MOE_BM)
        grp = (b >= e0).astype(jnp.int32) + (b >= e1).astype(jnp.int32) + (b >= e2).astype(jnp.int32)
        pb = jnp.where(r_iota + r0 == pos, 1.0, 0.0).astype(BF16)
        xb = _dot(pb, h).astype(BF16)
        gs = _dot_nt(pb, gmat)
        hg = _dot(xb, wg_ref[grp])
        hu = _dot(xb, wu_ref[grp])
        gate_cols = jnp.concatenate(
            [jnp.broadcast_to(gs[:, j:j + 1] + gs[:, PER_GROUP + j:PER_GROUP + j + 1], (MOE_BM, D_EXPERT))
             for j in range(PER_GROUP)], axis=-1)
        a = (_silu(hg) * hu * gate_cols).astype(BF16)
        ys_ref[pl.ds(r0, MOE_BM), :] = _dot(a, wd_ref[grp]).astype(BF16)

    n_iter = lax.shift_right_logical(n_blk + (MOE_UNROLL - 1), MOE_UNROLL.bit_length() - 1)

    def blocks(i, carry):
        for u in range(MOE_UNROLL):
            block(i * MOE_UNROLL + u)
        return carry

    lax.fori_loop(0, n_iter, blocks, 0)

    def clear(b, carry):
        r0 = pl.multiple_of(b * MOE_BM, MOE_BM)
        ys_ref[pl.ds(r0, MOE_BM), :] = jnp.zeros((MOE_BM, D_MODEL), BF16)
        return carry

    lax.fori_loop(n_iter * MOE_UNROLL, MOE_R // MOE_BM, clear, 0)

    pos_col = jnp.broadcast_to(posf, (LANES, tm)).T
    lane = lax.broadcasted_iota(jnp.int32, (tm, LANES), 1).astype(F32)
    pt = jnp.concatenate(
        [jnp.where(pos_col == lane + float(LANES * k), 1.0, 0.0).astype(BF16) for k in range(MOE_R // LANES)],
        axis=-1)
    y = x + _dot(pt, ys_ref[...])
    if final:
        y = _rmsnorm(y, gf_ref[...])
    o_ref[...] = y.reshape(s_t, l_t, D_MODEL)


def _moe(x3, layer, wts, experts, g_final, *, s_t, l_t, final):
    s_all, l_all, _ = x3.shape
    assert s_t * l_t == MOE_TM and MOE_R % (MOE_BM * MOE_UNROLL) == 0
    grid = (s_all // s_t, l_all // l_t)
    x_spec = pl.BlockSpec((s_t, l_t, D_MODEL), lambda b, j: (b, j, 0))
    names = ("g_ffn", "w_r", "b_r")
    return pl.pallas_call(
        functools.partial(_moe_kernel, s_t=s_t, l_t=l_t, final=final),
        grid=grid,
        in_specs=([x_spec] + [_layer_spec(wts[n], layer, pipeline_mode=pl.Buffered(1)) for n in names]
                  + [pl.BlockSpec(w.shape, lambda b, j: (0, 0, 0), pipeline_mode=pl.Buffered(1)) for w in experts]
                  + [pl.BlockSpec(g_final.shape, lambda b, j: (0, 0))]),
        out_specs=x_spec,
        out_shape=jax.ShapeDtypeStruct(x3.shape, F32),
        scratch_shapes=[pltpu.VMEM((MOE_R, D_MODEL), BF16)],
        compiler_params=_PARAMS,
        name="moe",
    )(x3, *[wts[n] for n in names], *experts, g_final)


def kernel(x_prompt, x_sample, cache_mem_k, cache_mem_v, state_pool, mem_prompt, g_mix, w_in, g_v, w_s, b_s, w_pool, s_pool, w_out, g_mem, g_xattn, w_xq, w_xk, w_xv, w_xo, g_ffn, w_group, b_group, w_router, b_router, w_gate, w_up, w_down, g_final):
    bp, seq, _ = x_prompt.shape
    bs, dseq, _ = x_sample.shape

    mem_k, mem_v, mem_kb, mem_vb, w_in_b, w_out_b, w_xq_b, w_xo_b = _memkv(
        mem_prompt.reshape(bp * N_MEM, D_MODEL), g_mem.reshape(1, D_MODEL), w_xk, w_xv, w_in, w_out, w_xq, w_xo)
    mem_kb = mem_kb.reshape(DEPTH * bp, N_MEM, D_MODEL)
    mem_vb = mem_vb.reshape(DEPTH * bp, N_MEM, D_MODEL)

    pad_r = ROUTER_LANES - N_GROUPS - N_EXPERTS
    bias_full = jnp.repeat(jnp.swapaxes(b_s, 1, 2), HEAD_DIM_A, axis=2)
    wts = {
        "g_mix": g_mix.reshape(DEPTH, 1, D_MODEL), "w_in": w_in_b, "g_v": g_v.reshape(DEPTH, 1, WIDTH_A),
        "w_s": w_s.astype(BF16), "bias_p": bias_full,
        "coef_s": jnp.repeat(jnp.transpose(w_s[:, :, :dseq, :dseq], (0, 3, 2, 1)), HEAD_DIM_A, axis=3),
        "bias_s": bias_full[:, :dseq, :],
        "w_pool": w_pool.astype(BF16), "s_pool": s_pool.reshape(DEPTH, 1, WIDTH_B), "w_out": w_out_b,
        "g_xattn": g_xattn.reshape(DEPTH, 1, D_MODEL), "w_xq": w_xq_b, "w_xo": w_xo_b,
        "g_ffn": g_ffn.reshape(DEPTH, 1, D_MODEL),
        "w_r": jnp.concatenate([w_group, w_router, jnp.zeros((DEPTH, D_MODEL, pad_r), F32)], axis=-1).astype(BF16),
        "b_r": jnp.concatenate([b_group, b_router, jnp.zeros((DEPTH, pad_r), F32)], axis=-1).reshape(DEPTH, 1, ROUTER_LANES),
        "w_gate": w_gate, "w_up": w_up, "w_down": w_down,
    }
    g_final2 = g_final.reshape(1, D_MODEL)

    yp, ys = x_prompt, x_sample
    cv_p, cv_s, pl_p, pl_s = [], [], [], []
    for l in range(DEPTH):
        final = l == DEPTH - 1

        yp, cvp, plp = _mix(yp, None, l, wts, s_t=1, l_t=PROMPT_TM, pos0=0)
        ys, cvs, pls = _mix(ys, state_pool, l, wts, s_t=SAMPLE_MIX_SEQS, l_t=dseq, pos0=PAST_LEN)
        yp, ys, *experts = _xattn(yp, ys, mem_kb, mem_vb, cache_mem_k, cache_mem_v, l, wts, l_p=PROMPT_TM)
        yp = _moe(yp, l, wts, experts, g_final2, s_t=1, l_t=MOE_TM, final=final)
        ys = _moe(ys, l, wts, experts, g_final2, s_t=MOE_TM // dseq, l_t=dseq, final=final)

        cv_p.append(cvp.reshape(bp, CHUNK, HEADS_A, HEAD_DIM_A))
        cv_s.append(cvs.reshape(bs, dseq, HEADS_A, HEAD_DIM_A))
        pl_p.append(plp)
        pl_s.append(pls)

    return (yp, ys, mem_k, mem_v, jnp.stack(cv_p), jnp.stack(cv_s), jnp.stack(pl_p), jnp.stack(pl_s))
```

```python
import functools

import jax
import jax.numpy as jnp
from jax import lax
from jax.experimental import pallas as pl
from jax.experimental.pallas import tpu as pltpu

D_MODEL = 1024
DEPTH = 4
PAST_LEN = 16384
CHUNK = 128
WIDTH_A = 512
WIDTH_B = 512
HEADS_A = 8
HEAD_DIM_A = 64
POOL_WINDOWS = (2, 4, 8, 16)
POOL_GROUP_DIM = 128
POOL_HIST = 15
HALO = 16
PROJ_WIDTH = 2 * WIDTH_A + WIDTH_B
N_MEM = 256
X_HEADS = 4
X_HEAD_DIM = 256
N_GROUPS = 4
PER_GROUP = 4
N_EXPERTS = 16
D_EXPERT = 256
EPS = 1e-6
ROUTER_LANES = 128
LANES = 128
GATE_ROWS = 16

PROMPT_TM = 512
MIX_SUB = 256
SAMPLE_MIX_SEQS = 32
SAMPLE_ATT_SEQS = 8
MOE_TOK = 448
MOE_TM = 512
MOE_BM = 128
MOE_R = 1024
MOE_UNROLL = 2

F32 = jnp.float32
BF16 = jnp.bfloat16

VMEM_LIMIT_BYTES = 56 * 1024 * 1024


def _rmsnorm(x, g):
    r = lax.rsqrt(jnp.mean(x * x, axis=-1, keepdims=True) + EPS)
    return x * r * g


def _gelu_tanh(x):
    c = 0.7978845608028654
    return 0.5 * x * (1.0 + jnp.tanh(c * (x + 0.044715 * (x * x * x))))


def _silu(x):
    return x * (1.0 / (1.0 + jnp.exp(-x)))


def _dot(a, b):
    return jnp.dot(a, b, preferred_element_type=F32)


def _dot_nt(a, b):
    return lax.dot_general(a, b, (((1,), (1,)), ((), ())), preferred_element_type=F32)


def _layer_spec(arr, layer, **kw):
    nd = arr.ndim - 1
    return pl.BlockSpec((None,) + arr.shape[1:], lambda *_: (layer,) + (0,) * nd, **kw)


_PARAMS = pltpu.CompilerParams(dimension_semantics=("arbitrary", "arbitrary"), vmem_limit_bytes=VMEM_LIMIT_BYTES)


def _memkv_kernel(mem_ref, g_ref, wk_ref, wv_ref, win_ref, wout_ref, wq_ref, wo_ref,
                  k_ref, v_ref, kb_ref, vb_ref, win_o, wout_o, wq_o, wo_o, wkb_ref, wvb_ref):
    win_o[...] = win_ref[...].astype(BF16)
    wout_o[...] = wout_ref[...].astype(BF16)
    wq_o[...] = wq_ref[...].astype(BF16)
    wo_o[...] = wo_ref[...].astype(BF16)

    @pl.when(pl.program_id(1) == 0)
    def _():
        wkb_ref[...] = wk_ref[0].astype(BF16)
        wvb_ref[...] = wv_ref[0].astype(BF16)

    h = _rmsnorm(mem_ref[...], g_ref[...]).astype(BF16)
    k = _dot(h, wkb_ref[...])
    v = _dot(h, wvb_ref[...])
    for hh in range(X_HEADS):
        sl = slice(hh * X_HEAD_DIM, (hh + 1) * X_HEAD_DIM)
        k_ref[0, 0, :, hh, :] = k[:, sl]
        v_ref[0, 0, :, hh, :] = v[:, sl]
    kb_ref[0] = k.astype(BF16)
    vb_ref[0] = v.astype(BF16)


def _memkv(mem2d, g_mem, wk, wv, w_in, w_out, w_xq, w_xo):
    rows = mem2d.shape[0]
    n_r = rows // N_MEM
    slab = D_MODEL // n_r
    out_sds = jax.ShapeDtypeStruct((DEPTH, n_r, N_MEM, X_HEADS, X_HEAD_DIM), F32)
    outb_sds = jax.ShapeDtypeStruct((DEPTH, rows, D_MODEL), BF16)
    w_spec = pl.BlockSpec((1, D_MODEL, D_MODEL), lambda l, r: (l, 0, 0))
    o5_spec = pl.BlockSpec((1, 1, N_MEM, X_HEADS, X_HEAD_DIM), lambda l, r: (l, r, 0, 0, 0))
    o_spec = pl.BlockSpec((1, N_MEM, D_MODEL), lambda l, r: (l, r, 0))
    dense = (w_in, w_out, w_xq, w_xo)
    slab_specs = [pl.BlockSpec((1, slab, w.shape[2]), lambda l, r: (l, r, 0)) for w in dense]
    return pl.pallas_call(
        _memkv_kernel,
        grid=(DEPTH, n_r),
        in_specs=[pl.BlockSpec((N_MEM, D_MODEL), lambda l, r: (r, 0)),
                  pl.BlockSpec((1, D_MODEL), lambda l, r: (0, 0)),
                  w_spec, w_spec] + slab_specs,
        out_specs=[o5_spec, o5_spec, o_spec, o_spec] + slab_specs,
        out_shape=[out_sds, out_sds, outb_sds, outb_sds] + [jax.ShapeDtypeStruct(w.shape, BF16) for w in dense],
        scratch_shapes=[pltpu.VMEM((D_MODEL, D_MODEL), BF16), pltpu.VMEM((D_MODEL, D_MODEL), BF16)],
        compiler_params=_PARAMS,
        name="memkv",
    )(mem2d, g_mem, wk, wv, *dense)


def _gate_chunks(v, ws_ref, tm):
    r_i = lax.broadcasted_iota(jnp.int32, (CHUNK, CHUNK), 0)
    c_i = lax.broadcasted_iota(jnp.int32, (CHUNK, CHUNK), 1)
    causal = c_i <= r_i
    lane = lax.broadcasted_iota(jnp.int32, (CHUNK, LANES), 1)
    zero_b = jnp.zeros((), BF16)
    vb = v.astype(BF16)
    pair_w = []
    for p in range(HEADS_A // 2):
        m_lo = jnp.where(causal, ws_ref[2 * p], zero_b)
        m_hi = jnp.where(causal, ws_ref[2 * p + 1], zero_b)
        pair_w.append(jnp.concatenate([m_lo, m_hi], axis=1))
    rows = []
    for c in range(tm // CHUNK):
        cols = []
        for p in range(HEADS_A // 2):
            vp = vb[c * CHUNK:(c + 1) * CHUNK, LANES * p:LANES * (p + 1)]
            lo = jnp.where(lane < HEAD_DIM_A, vp, zero_b)
            hi = jnp.where(lane >= HEAD_DIM_A, vp, zero_b)
            cols.append(_dot(pair_w[p], jnp.concatenate([lo, hi], axis=0)))
        rows.append(jnp.concatenate(cols, axis=-1))
    return jnp.concatenate(rows, axis=0)


def _gate_short(v3, coef_ref, l_t):
    i_i = lax.broadcasted_iota(jnp.int32, (l_t, WIDTH_A), 0)
    acc = None
    for jj in range(l_t):
        cj = jnp.where(i_i >= jj, coef_ref[jj], 0.0)
        term = cj[None, :, :] * v3[:, jj:jj + 1, :]
        acc = term if acc is None else acc + term
    return acc


def _trailing_sums(zx, n_new):
    outs = []
    for g, w in enumerate(POOL_WINDOWS):
        a = zx[:, :, g * POOL_GROUP_DIM:(g + 1) * POOL_GROUP_DIM]
        span = 1
        while span < w:
            n = a.shape[1]
            a = a[:, span:, :] + a[:, :n - span, :]
            span *= 2
        first = HALO + 1 - w
        outs.append(a[:, first:first + n_new, :])
    return outs


def _mix_kernel(*refs, s_t, l_t, pos0, has_hist):
    if has_hist:
        (x_ref, hist_ref, gmix_ref, win_ref, gv_ref, ws_ref, mb_ref, wpool_ref, spool_ref, wout_ref,
         x1_ref, v_ref, zt_ref, zext_ref) = refs
    else:
        (x_ref, gmix_ref, win_ref, gv_ref, ws_ref, mb_ref, wpool_ref, spool_ref, wout_ref,
         x1_ref, v_ref, zt_ref, zext_ref, carry_ref) = refs
    j = pl.program_id(1)
    sub = min(l_t, MIX_SUB)
    ts = s_t * sub

    if has_hist:
        zext_ref[:, HALO - POOL_HIST:HALO, :] = hist_ref[...]
    else:
        @pl.when(j == 0)
        def _():
            zext_ref[:, 0:HALO, :] = jnp.zeros((s_t, HALO, WIDTH_B), F32)

        @pl.when(j > 0)
        def _():
            zext_ref[:, 0:HALO, :] = carry_ref[...]

    for r0 in range(0, l_t, sub):
        x = x_ref[:, r0:r0 + sub, :].reshape(ts, D_MODEL)
        h = _rmsnorm(x, gmix_ref[...]).astype(BF16)
        proj = _dot(h, win_ref[...])
        ua = _gelu_tanh(proj[:, :2 * WIDTH_A])
        u = ua[:, :WIDTH_A]
        v = _rmsnorm(ua[:, WIDTH_A:], gv_ref[...])
        z3 = proj[:, 2 * WIDTH_A:].reshape(s_t, sub, WIDTH_B)

        if has_hist:
            mixed = _gate_short(v.reshape(s_t, sub, WIDTH_A), ws_ref, sub) + mb_ref[...][None, :, :]
        else:
            mixed = _gate_chunks(v, ws_ref, ts).reshape(ts // CHUNK, CHUNK, WIDTH_A) + mb_ref[...][None, :, :]
        a_out = u * mixed.reshape(ts, WIDTH_A)

        zext_ref[:, HALO + r0:HALO + r0 + sub, :] = z3
        sums = _trailing_sums(zext_ref[:, r0:r0 + HALO + sub, :], sub)
        pos = pos0 + j * l_t + r0 + lax.broadcasted_iota(jnp.int32, (1, sub, 1), 1)
        pooled_out = []
        for g, w in enumerate(POOL_WINDOWS):
            sl = slice(g * POOL_GROUP_DIM, (g + 1) * POOL_GROUP_DIM)
            inv_cnt = 1.0 / jnp.minimum(w, pos + 1).astype(F32)
            pooled = sums[g] * inv_cnt - z3[:, :, sl]
            pooled_out.append(_dot(pooled.reshape(ts, POOL_GROUP_DIM).astype(BF16), wpool_ref[g]))
        b_out = jnp.concatenate(pooled_out, axis=-1) * spool_ref[...]

        cat = jnp.concatenate([a_out, b_out], axis=-1).astype(BF16)
        x1 = x + _dot(cat, wout_ref[...])
        x1_ref[:, r0:r0 + sub, :] = x1.reshape(s_t, sub, D_MODEL)
        if has_hist:
            v_ref[...] = v.reshape(s_t, sub, WIDTH_A)
        elif r0 + sub == l_t:
            v_ref[...] = v[ts - CHUNK:, :].reshape(1, CHUNK, WIDTH_A)

    zt_ref[...] = zext_ref[:, l_t + 1:l_t + HALO, :]
    if not has_hist:
        carry_ref[...] = zext_ref[:, l_t:l_t + HALO, :]


def _mix(x3, hist, layer, wts, *, s_t, l_t, pos0):
    s_all, l_all, _ = x3.shape
    has_hist = hist is not None
    grid = (s_all // s_t, l_all // l_t)
    x_spec = pl.BlockSpec((s_t, l_t, D_MODEL), lambda b, j: (b, j, 0))
    in_specs = [x_spec]
    args = [x3]
    if has_hist:
        in_specs.append(pl.BlockSpec((None, s_t, POOL_HIST, WIDTH_B), lambda b, j: (layer, b, 0, 0)))
        args.append(hist)
    gate_w = ("coef_s", "bias_s") if has_hist else ("w_s", "bias_p")
    for name in ("g_mix", "w_in", "g_v") + gate_w + ("w_pool", "s_pool", "w_out"):
        in_specs.append(_layer_spec(wts[name], layer))
        args.append(wts[name])
    if has_hist:
        v_shape, v_blk = (s_all, l_all, WIDTH_A), (s_t, l_t, WIDTH_A)
    else:
        v_shape, v_blk = (s_all, CHUNK, WIDTH_A), (1, CHUNK, WIDTH_A)
    out_shape = [jax.ShapeDtypeStruct(x3.shape, F32),
                 jax.ShapeDtypeStruct(v_shape, F32),
                 jax.ShapeDtypeStruct((s_all, POOL_HIST, WIDTH_B), F32)]
    out_specs = [x_spec,
                 pl.BlockSpec(v_blk, lambda b, j: (b, 0, 0)),
                 pl.BlockSpec((s_t, POOL_HIST, WIDTH_B), lambda b, j: (b, 0, 0))]
    scratch = [pltpu.VMEM((s_t, HALO + l_t, WIDTH_B), F32)]
    if not has_hist:
        scratch.append(pltpu.VMEM((s_t, HALO, WIDTH_B), F32))
    return pl.pallas_call(
        functools.partial(_mix_kernel, s_t=s_t, l_t=l_t, pos0=pos0, has_hist=has_hist),
        grid=grid, in_specs=in_specs, out_specs=out_specs, out_shape=out_shape,
        scratch_shapes=scratch, compiler_params=_PARAMS,
        name="mix_sample" if has_hist else "mix_prompt",
    )(*args)


def _softmax_rows(sc):
    m = jnp.max(sc, axis=-1, keepdims=True)
    e = jnp.exp(sc - m)
    return e * (1.0 / jnp.sum(e, axis=-1, keepdims=True))


def _attend_split(qs, k_ref, v_ref, s):
    heads = []
    for hh in range(X_HEADS):
        sl = slice(hh * X_HEAD_DIM, (hh + 1) * X_HEAD_DIM)
        p = _softmax_rows(_dot_nt(qs[:, sl].astype(BF16), k_ref[s, :, sl]))
        heads.append(_dot(p.astype(BF16), v_ref[s, :, sl]))
    return jnp.concatenate(heads, axis=-1)


def _attend_native(qs, k_ref, v_ref, s, l_t):
    q_rows = jnp.concatenate([qs[:, hh * X_HEAD_DIM:(hh + 1) * X_HEAD_DIM] for hh in range(X_HEADS)], axis=0)
    k2 = k_ref[0, s].reshape(N_MEM * X_HEADS, X_HEAD_DIM).astype(BF16)
    v2 = v_ref[0, s].reshape(N_MEM * X_HEADS, X_HEAD_DIM).astype(BF16)
    sc = _dot_nt(q_rows.astype(BF16), k2)
    r_i = lax.broadcasted_iota(jnp.int32, sc.shape, 0)
    c_i = lax.broadcasted_iota(jnp.int32, sc.shape, 1)
    same_head = (r_i // l_t) == (c_i % X_HEADS)
    p = _softmax_rows(jnp.where(same_head, sc, -jnp.inf))
    o_rows = _dot(p.astype(BF16), v2)
    return jnp.concatenate([o_rows[hh * l_t:(hh + 1) * l_t] for hh in range(X_HEADS)], axis=-1)


def _xattn_kernel(xp_ref, xs_ref, kp_ref, vp_ref, ks_ref, vs_ref, g_ref, wq_ref, wo_ref, eg_ref, eu_ref, ed_ref,
                  op_ref, os_ref, eg_o, eu_o, ed_o, *, l_p, s_s, l_s):
    eg_o[...] = eg_ref[...].astype(BF16)
    eu_o[...] = eu_ref[...].astype(BF16)
    ed_o[...] = ed_ref[...].astype(BF16)
    n_s = s_s * l_s
    x = jnp.concatenate([xp_ref[...].reshape(l_p, D_MODEL), xs_ref[...].reshape(n_s, D_MODEL)], axis=0)
    h = _rmsnorm(x, g_ref[...]).astype(BF16)
    q = _dot(h, wq_ref[...]) * (X_HEAD_DIM ** -0.5)
    outs = [_attend_split(q[:l_p], kp_ref, vp_ref, 0)]
    for s in range(s_s):
        outs.append(_attend_native(q[l_p + s * l_s:l_p + (s + 1) * l_s], ks_ref, vs_ref, s, l_s))
    o = jnp.concatenate(outs, axis=0)
    x2 = x + _dot(o.astype(BF16), wo_ref[...])
    op_ref[...] = x2[:l_p].reshape(1, l_p, D_MODEL)
    os_ref[...] = x2[l_p:].reshape(s_s, l_s, D_MODEL)


def _xattn(xp, xs, mem_kb, mem_vb, cache_k, cache_v, layer, wts, *, l_p):
    bp, seq, _ = xp.shape
    s_all, l_s, _ = xs.shape
    tiles = seq // l_p
    steps = bp * tiles
    s_s = s_all // steps
    parts = steps // N_EXPERTS
    assert s_s * steps == s_all and seq % l_p == 0 and parts * N_EXPERTS == steps
    rows_in, rows_mid = D_MODEL // parts, D_EXPERT // parts

    def step(b, j):
        return b * tiles + j

    xp_spec = pl.BlockSpec((1, l_p, D_MODEL), lambda b, j: (b, j, 0))
    xs_spec = pl.BlockSpec((s_s, l_s, D_MODEL), lambda b, j: (step(b, j), 0, 0))
    mp_spec = pl.BlockSpec((1, N_MEM, D_MODEL), lambda b, j: (layer * bp + b, 0, 0))
    ms_spec = pl.BlockSpec((1, s_s, N_MEM, X_HEADS, X_HEAD_DIM), lambda b, j: (layer, step(b, j), 0, 0, 0))
    gu_in = pl.BlockSpec((None, None, rows_in, D_EXPERT),
                         lambda b, j: (layer, step(b, j) // parts, step(b, j) % parts, 0))
    dn_in = pl.BlockSpec((None, None, rows_mid, D_MODEL),
                         lambda b, j: (layer, step(b, j) // parts, step(b, j) % parts, 0))
    gu_out = pl.BlockSpec((None, rows_in, D_EXPERT),
                          lambda b, j: (step(b, j) // parts // PER_GROUP, step(b, j) % parts,
                                        step(b, j) // parts % PER_GROUP))
    dn_out = pl.BlockSpec((None, rows_mid, D_MODEL),
                          lambda b, j: (step(b, j) // parts // PER_GROUP,
                                        (step(b, j) // parts % PER_GROUP) * parts + step(b, j) % parts, 0))
    names = ("g_xattn", "w_xq", "w_xo")
    gu_sds = jax.ShapeDtypeStruct((N_GROUPS, D_MODEL, PER_GROUP * D_EXPERT), BF16)
    dn_sds = jax.ShapeDtypeStruct((N_GROUPS, PER_GROUP * D_EXPERT, D_MODEL), BF16)
    return pl.pallas_call(
        functools.partial(_xattn_kernel, l_p=l_p, s_s=s_s, l_s=l_s),
        grid=(bp, tiles),
        in_specs=([xp_spec, xs_spec, mp_spec, mp_spec, ms_spec, ms_spec] + [_layer_spec(wts[n], layer) for n in names]
                  + [gu_in, gu_in, dn_in]),
        out_specs=[xp_spec, xs_spec, gu_out, gu_out, dn_out],
        out_shape=[jax.ShapeDtypeStruct(xp.shape, F32), jax.ShapeDtypeStruct(xs.shape, F32), gu_sds, gu_sds, dn_sds],
        compiler_params=_PARAMS,
        name="xattn",
    )(xp, xs, mem_kb, mem_vb, cache_k, cache_v, *[wts[n] for n in names],
      wts["w_gate"], wts["w_up"], wts["w_down"])


def _first_index(vals, target):
    idx = jnp.full(target.shape, len(vals) - 1, jnp.int32)
    for i in range(len(vals) - 2, -1, -1):
        idx = jnp.where(vals[i] == target, i, idx)
    return idx


def _max_of(vals):
    m = vals[0]
    for v in vals[1:]:
        m = jnp.maximum(m, v)
    return m


def _route(lt):
    rows = [lt[i:i + 1, :] for i in range(N_GROUPS + N_EXPERTS)]
    g_l = rows[:N_GROUPS]
    gmax = _max_of(g_l)
    gsum = g_l[0] * 0.0
    for v in g_l:
        gsum = gsum + jnp.exp(v - gmax)
    g_p = 1.0 / gsum
    g_idx = _first_index(g_l, gmax)
    e_sel = []
    for j in range(PER_GROUP):
        v = rows[N_GROUPS + PER_GROUP * (N_GROUPS - 1) + j]
        for g in range(N_GROUPS - 2, -1, -1):
            v = jnp.where(g_idx == g, rows[N_GROUPS + PER_GROUP * g + j], v)
        e_sel.append(v)
    v1 = _max_of(e_sel)
    i1 = _first_index(e_sel, v1)
    neg = jnp.float32(-jnp.inf)
    e_rest = [jnp.where(i1 == j, neg, e_sel[j]) for j in range(PER_GROUP)]
    v2 = _max_of(e_rest)
    i2 = _first_index(e_rest, v2)
    t = jnp.exp(v2 - v1)
    w1 = g_p / (1.0 + t)
    w2 = g_p * t / (1.0 + t)
    gates = [jnp.where(i1 == j, w1, 0.0) + jnp.where(i2 == j, w2, 0.0) for j in range(PER_GROUP)]
    return g_idx, gates


def _stack_rows(rows, n_rows, tm):
    ri = lax.broadcasted_iota(jnp.int32, (n_rows, tm), 0)
    out = jnp.zeros((n_rows, tm), F32)
    for i, r in enumerate(rows):
        out = jnp.where(ri == i, r, out)
    return out


def _moe_kernel(x_ref, g_ref, wr_ref, br_ref, wg_ref, wu_ref, wd_ref, gf_ref, o_ref, ys_ref, *, final):
    tm = MOE_TM
    x = jnp.concatenate([x_ref[...], jnp.zeros((MOE_TM - MOE_TOK, D_MODEL), F32)], axis=0)
    h = _rmsnorm(x, g_ref[...]).astype(BF16)
    logits = _dot(h, wr_ref[...]) + br_ref[...]
    g_idx, gates = _route(logits.T)

    real = lax.broadcasted_iota(jnp.int32, (1, tm), 1) < MOE_TOK
    member = [jnp.where(jnp.logical_and(g_idx == g, real), 1.0, 0.0) for g in range(N_GROUPS)]
    t_r = lax.broadcasted_iota(jnp.int32, (tm, tm), 0)
    t_c = lax.broadcasted_iota(jnp.int32, (tm, tm), 1)
    upper = jnp.where(t_r <= t_c, 1.0, 0.0).astype(BF16)
    incl = _dot(_stack_rows(member, GATE_ROWS, tm).astype(BF16), upper)
    end_blk = []
    posf = jnp.zeros((1, tm), F32)
    run = jnp.zeros((1, 1), F32)
    for g in range(N_GROUPS):
        n_g = jnp.sum(member[g], axis=-1, keepdims=True)
        posf = posf + member[g] * (run * MOE_BM + incl[g:g + 1, :] - 1.0)
        run = run + jnp.floor((n_g + (MOE_BM - 1)) * (1.0 / MOE_BM))
        end_blk.append(run)
    posf = jnp.where(real, posf, -1.0)
    pos = posf.astype(jnp.int32)
    e0, e1, e2, n_blk = [e.astype(jnp.int32)[0, 0] for e in end_blk]

    g_hi = [gt.astype(BF16).astype(F32) for gt in gates]
    g_lo = [gt - hi for gt, hi in zip(gates, g_hi)]
    gmat = _stack_rows(g_hi + g_lo, GATE_ROWS, tm).astype(BF16)

    r_iota = lax.broadcasted_iota(jnp.int32, (MOE_BM, tm), 0)

    def block(b):
        r0 = pl.multiple_of(b * MOE_BM, MOE_BM)
        grp = (b >= e0).astype(jnp.int32) + (b >= e1).astype(jnp.int32) + (b >= e2).astype(jnp.int32)
        pb = jnp.where(r_iota + r0 == pos, 1.0, 0.0).astype(BF16)
        xb = _dot(pb, h).astype(BF16)
        gs = _dot_nt(pb, gmat)
        hg = _dot(xb, wg_ref[grp])
        hu = _dot(xb, wu_ref[grp])
        gate_cols = jnp.concatenate(
            [jnp.broadcast_to(gs[:, j:j + 1] + gs[:, PER_GROUP + j:PER_GROUP + j + 1], (MOE_BM, D_EXPERT))
             for j in range(PER_GROUP)], axis=-1)
        a = (_silu(hg) * hu * gate_cols).astype(BF16)
        ys_ref[pl.ds(r0, MOE_BM), :] = _dot(a, wd_ref[grp]).astype(BF16)

    n_iter = lax.shift_right_logical(n_blk + (MOE_UNROLL - 1), MOE_UNROLL.bit_length() - 1)

    def blocks(i, carry):
        for u in range(MOE_UNROLL):
            block(i * MOE_UNROLL + u)
        return carry

    lax.fori_loop(0, n_iter, blocks, 0)

    def clear(b, carry):
        r0 = pl.multiple_of(b * MOE_BM, MOE_BM)
        ys_ref[pl.ds(r0, MOE_BM), :] = jnp.zeros((MOE_BM, D_MODEL), BF16)
        return carry

    pos_col = jnp.broadcast_to(posf, (LANES, tm)).T
    lane = lax.broadcasted_iota(jnp.int32, (tm, LANES), 1).astype(F32)

    def unsort(width):
        lax.fori_loop(n_iter * MOE_UNROLL, width // MOE_BM, clear, 0)
        pt = jnp.concatenate(
            [jnp.where(pos_col == lane + float(LANES * k), 1.0, 0.0).astype(BF16) for k in range(width // LANES)],
            axis=-1)
        y = x + _dot(pt, ys_ref[0:width, :])
        if final:
            y = _rmsnorm(y, gf_ref[...])
        o_ref[...] = y[:MOE_TOK]

    few = n_iter * (MOE_UNROLL * MOE_BM) <= MOE_R // 2

    @pl.when(few)
    def _():
        unsort(MOE_R // 2)

    @pl.when(jnp.logical_not(few))
    def _():
        unsort(MOE_R)


def _moe(x2, layer, wts, experts, g_final, *, final):
    n_rows = x2.shape[0]
    unit = 8
    assert MOE_R % (MOE_BM * MOE_UNROLL) == 0 and MOE_TOK + N_GROUPS * (MOE_BM - 1) <= MOE_R
    assert MOE_TOK % unit == 0 and n_rows % unit == 0 and n_rows >= MOE_TOK
    x_spec = pl.BlockSpec(
        (pl.Element(MOE_TOK), pl.Element(D_MODEL)),
        lambda i: (jnp.minimum(i * (MOE_TOK // unit), (n_rows - MOE_TOK) // unit) * unit, 0))
    names = ("g_ffn", "w_r", "b_r")
    return pl.pallas_call(
        functools.partial(_moe_kernel, final=final),
        grid=(pl.cdiv(n_rows, MOE_TOK),),
        in_specs=([x_spec] + [_layer_spec(wts[n], layer, pipeline_mode=pl.Buffered(1)) for n in names]
                  + [pl.BlockSpec(w.shape, lambda i: (0, 0, 0), pipeline_mode=pl.Buffered(1)) for w in experts]
                  + [pl.BlockSpec(g_final.shape, lambda i: (0, 0))]),
        out_specs=x_spec,
        out_shape=jax.ShapeDtypeStruct(x2.shape, F32),
        scratch_shapes=[pltpu.VMEM((MOE_R, D_MODEL), BF16)],
        compiler_params=pltpu.CompilerParams(dimension_semantics=("arbitrary",), vmem_limit_bytes=VMEM_LIMIT_BYTES),
        name="moe",
    )(x2, *[wts[n] for n in names], *experts, g_final)


def kernel(x_prompt, x_sample, cache_mem_k, cache_mem_v, state_pool, mem_prompt, g_mix, w_in, g_v, w_s, b_s, w_pool, s_pool, w_out, g_mem, g_xattn, w_xq, w_xk, w_xv, w_xo, g_ffn, w_group, b_group, w_router, b_router, w_gate, w_up, w_down, g_final):
    bp, seq, _ = x_prompt.shape
    bs, dseq, _ = x_sample.shape

    mem_k, mem_v, mem_kb, mem_vb, w_in_b, w_out_b, w_xq_b, w_xo_b = _memkv(
        mem_prompt.reshape(bp * N_MEM, D_MODEL), g_mem.reshape(1, D_MODEL), w_xk, w_xv, w_in, w_out, w_xq, w_xo)
    mem_kb = mem_kb.reshape(DEPTH * bp, N_MEM, D_MODEL)
    mem_vb = mem_vb.reshape(DEPTH * bp, N_MEM, D_MODEL)

    pad_r = ROUTER_LANES - N_GROUPS - N_EXPERTS
    bias_full = jnp.repeat(jnp.swapaxes(b_s, 1, 2), HEAD_DIM_A, axis=2)
    wts = {
        "g_mix": g_mix.reshape(DEPTH, 1, D_MODEL), "w_in": w_in_b, "g_v": g_v.reshape(DEPTH, 1, WIDTH_A),
        "w_s": w_s.astype(BF16), "bias_p": bias_full,
        "coef_s": jnp.repeat(jnp.transpose(w_s[:, :, :dseq, :dseq], (0, 3, 2, 1)), HEAD_DIM_A, axis=3),
        "bias_s": bias_full[:, :dseq, :],
        "w_pool": w_pool.astype(BF16), "s_pool": s_pool.reshape(DEPTH, 1, WIDTH_B), "w_out": w_out_b,
        "g_xattn": g_xattn.reshape(DEPTH, 1, D_MODEL), "w_xq": w_xq_b, "w_xo": w_xo_b,
        "g_ffn": g_ffn.reshape(DEPTH, 1, D_MODEL),
        "w_r": jnp.concatenate([w_group, w_router, jnp.zeros((DEPTH, D_MODEL, pad_r), F32)], axis=-1).astype(BF16),
        "b_r": jnp.concatenate([b_group, b_router, jnp.zeros((DEPTH, pad_r), F32)], axis=-1).reshape(DEPTH, 1, ROUTER_LANES),
        "w_gate": w_gate, "w_up": w_up, "w_down": w_down,
    }
    g_final2 = g_final.reshape(1, D_MODEL)

    yp, ys = x_prompt, x_sample
    cv_p, cv_s, pl_p, pl_s = [], [], [], []
    for l in range(DEPTH):
        final = l == DEPTH - 1

        yp, cvp, plp = _mix(yp, None, l, wts, s_t=1, l_t=PROMPT_TM, pos0=0)
        ys, cvs, pls = _mix(ys, state_pool, l, wts, s_t=SAMPLE_MIX_SEQS, l_t=dseq, pos0=PAST_LEN)
        yp, ys, *experts = _xattn(yp, ys, mem_kb, mem_vb, cache_mem_k, cache_mem_v, l, wts, l_p=PROMPT_TM)
        yp = _moe(yp.reshape(bp * seq, D_MODEL), l, wts, experts, g_final2, final=final).reshape(bp, seq, D_MODEL)
        ys = _moe(ys.reshape(bs * dseq, D_MODEL), l, wts, experts, g_final2, final=final).reshape(bs, dseq, D_MODEL)

        cv_p.append(cvp.reshape(bp, CHUNK, HEADS_A, HEAD_DIM_A))
        cv_s.append(cvs.reshape(bs, dseq, HEADS_A, HEAD_DIM_A))
        pl_p.append(plp)
        pl_s.append(pls)

    return (yp, ys, mem_k, mem_v, jnp.stack(cv_p), jnp.stack(cv_s), jnp.stack(pl_p), jnp.stack(pl_s))
```

```python
import functools

import jax
import jax.numpy as jnp
from jax import lax
from jax.experimental import pallas as pl
from jax.experimental.pallas import tpu as pltpu

D_MODEL = 1024
DEPTH = 4
PAST_LEN = 16384
CHUNK = 128
WIDTH_A = 512
WIDTH_B = 512
HEADS_A = 8
HEAD_DIM_A = 64
POOL_WINDOWS = (2, 4, 8, 16)
POOL_GROUP_DIM = 128
POOL_HIST = 15
HALO = 16
PROJ_WIDTH = 2 * WIDTH_A + WIDTH_B
N_MEM = 256
X_HEADS = 4
X_HEAD_DIM = 256
N_GROUPS = 4
PER_GROUP = 4
N_EXPERTS = 16
D_EXPERT = 256
EPS = 1e-6
ROUTER_LANES = 128
LANES = 128
GATE_ROWS = 16

MEMKV_SEQS = 2
PROMPT_TM = 512
MIX_TM = 1024
MIX_SUB = 256
SAMPLE_MIX_SEQS = 32
SAMPLE_ATT_SEQS = 8
MOE_TM = 512
MOE_BM = 128
MOE_R = 1024
MOE_UNROLL = 2

F32 = jnp.float32
BF16 = jnp.bfloat16

VMEM_LIMIT_BYTES = 56 * 1024 * 1024


def _rmsnorm(x, g):
    r = lax.rsqrt(jnp.mean(x * x, axis=-1, keepdims=True) + EPS)
    return x * r * g


def _gelu_tanh(x):
    c = 0.7978845608028654
    return 0.5 * x * (1.0 + jnp.tanh(c * (x + 0.044715 * (x * x * x))))


def _silu(x):
    return x * (1.0 / (1.0 + jnp.exp(-x)))


def _dot(a, b):
    return jnp.dot(a, b, preferred_element_type=F32)


def _dot_nt(a, b):
    return lax.dot_general(a, b, (((1,), (1,)), ((), ())), preferred_element_type=F32)


def _layer_spec(arr, layer, **kw):
    nd = arr.ndim - 1
    return pl.BlockSpec((None,) + arr.shape[1:], lambda *_: (layer,) + (0,) * nd, **kw)


_PARAMS = pltpu.CompilerParams(dimension_semantics=("arbitrary", "arbitrary"), vmem_limit_bytes=VMEM_LIMIT_BYTES)


def _memkv_kernel(mem_ref, g_ref, wk_ref, wv_ref, win_ref, wout_ref, wq_ref, wo_ref,
                  k_ref, v_ref, kb_ref, vb_ref, win_o, wout_o, wq_o, wo_o, wkb_ref, wvb_ref):
    win_o[...] = win_ref[...].astype(BF16)
    wout_o[...] = wout_ref[...].astype(BF16)
    wq_o[...] = wq_ref[...].astype(BF16)
    wo_o[...] = wo_ref[...].astype(BF16)

    @pl.when(pl.program_id(1) == 0)
    def _():
        wkb_ref[...] = wk_ref[0].astype(BF16)
        wvb_ref[...] = wv_ref[0].astype(BF16)

    h = _rmsnorm(mem_ref[...], g_ref[...]).astype(BF16)
    k = _dot(h, wkb_ref[...])
    v = _dot(h, wvb_ref[...])
    for s in range(MEMKV_SEQS):
        rows = slice(s * N_MEM, (s + 1) * N_MEM)
        for hh in range(X_HEADS):
            sl = slice(hh * X_HEAD_DIM, (hh + 1) * X_HEAD_DIM)
            k_ref[0, s, :, hh, :] = k[rows, sl]
            v_ref[0, s, :, hh, :] = v[rows, sl]
    kb_ref[0] = k.astype(BF16)
    vb_ref[0] = v.astype(BF16)


def _memkv(mem2d, g_mem, wk, wv, w_in, w_out, w_xq, w_xo):
    rows = mem2d.shape[0]
    tile = MEMKV_SEQS * N_MEM
    n_r = rows // tile
    slab = D_MODEL // n_r
    assert n_r * tile == rows and slab * n_r == D_MODEL
    out_sds = jax.ShapeDtypeStruct((DEPTH, rows // N_MEM, N_MEM, X_HEADS, X_HEAD_DIM), F32)
    outb_sds = jax.ShapeDtypeStruct((DEPTH, rows, D_MODEL), BF16)
    w_spec = pl.BlockSpec((1, D_MODEL, D_MODEL), lambda l, r: (l, 0, 0))
    o5_spec = pl.BlockSpec((1, MEMKV_SEQS, N_MEM, X_HEADS, X_HEAD_DIM), lambda l, r: (l, r, 0, 0, 0))
    o_spec = pl.BlockSpec((1, tile, D_MODEL), lambda l, r: (l, r, 0))
    dense = (w_in, w_out, w_xq, w_xo)
    slab_specs = [pl.BlockSpec((1, slab, w.shape[2]), lambda l, r: (l, r, 0)) for w in dense]
    return pl.pallas_call(
        _memkv_kernel,
        grid=(DEPTH, n_r),
        in_specs=[pl.BlockSpec((tile, D_MODEL), lambda l, r: (r, 0)),
                  pl.BlockSpec((1, D_MODEL), lambda l, r: (0, 0)),
                  w_spec, w_spec] + slab_specs,
        out_specs=[o5_spec, o5_spec, o_spec, o_spec] + slab_specs,
        out_shape=[out_sds, out_sds, outb_sds, outb_sds] + [jax.ShapeDtypeStruct(w.shape, BF16) for w in dense],
        scratch_shapes=[pltpu.VMEM((D_MODEL, D_MODEL), BF16), pltpu.VMEM((D_MODEL, D_MODEL), BF16)],
        compiler_params=_PARAMS,
        name="memkv",
    )(mem2d, g_mem, wk, wv, *dense)


def _gate_chunks(v, ws_ref, tm):
    r_i = lax.broadcasted_iota(jnp.int32, (CHUNK, CHUNK), 0)
    c_i = lax.broadcasted_iota(jnp.int32, (CHUNK, CHUNK), 1)
    causal = c_i <= r_i
    lane = lax.broadcasted_iota(jnp.int32, (CHUNK, LANES), 1)
    zero_b = jnp.zeros((), BF16)
    vb = v.astype(BF16)
    pair_w = []
    for p in range(HEADS_A // 2):
        m_lo = jnp.where(causal, ws_ref[2 * p], zero_b)
        m_hi = jnp.where(causal, ws_ref[2 * p + 1], zero_b)
        pair_w.append(jnp.concatenate([m_lo, m_hi], axis=1))
    rows = []
    for c in range(tm // CHUNK):
        cols = []
        for p in range(HEADS_A // 2):
            vp = vb[c * CHUNK:(c + 1) * CHUNK, LANES * p:LANES * (p + 1)]
            lo = jnp.where(lane < HEAD_DIM_A, vp, zero_b)
            hi = jnp.where(lane >= HEAD_DIM_A, vp, zero_b)
            cols.append(_dot(pair_w[p], jnp.concatenate([lo, hi], axis=0)))
        rows.append(jnp.concatenate(cols, axis=-1))
    return jnp.concatenate(rows, axis=0)


def _gate_short(v3, coef_ref, l_t):
    i_i = lax.broadcasted_iota(jnp.int32, (l_t, WIDTH_A), 0)
    acc = None
    for jj in range(l_t):
        cj = jnp.where(i_i >= jj, coef_ref[jj], 0.0)
        term = cj[None, :, :] * v3[:, jj:jj + 1, :]
        acc = term if acc is None else acc + term
    return acc


def _trailing_sums(zx, n_new):
    outs = []
    for g, w in enumerate(POOL_WINDOWS):
        a = zx[:, :, g * POOL_GROUP_DIM:(g + 1) * POOL_GROUP_DIM]
        span = 1
        while span < w:
            n = a.shape[1]
            a = a[:, span:, :] + a[:, :n - span, :]
            span *= 2
        first = HALO + 1 - w
        outs.append(a[:, first:first + n_new, :])
    return outs


def _mix_kernel(*refs, s_t, l_t, pos0, has_hist):
    if has_hist:
        (x_ref, hist_ref, gmix_ref, win_ref, gv_ref, ws_ref, mb_ref, wpool_ref, spool_ref, wout_ref,
         x1_ref, v_ref, zt_ref, zext_ref) = refs
    else:
        (x_ref, gmix_ref, win_ref, gv_ref, ws_ref, mb_ref, wpool_ref, spool_ref, wout_ref,
         x1_ref, v_ref, zt_ref, zext_ref, carry_ref) = refs
    j = pl.program_id(1)
    sub = min(l_t, MIX_SUB)
    ts = s_t * sub

    if has_hist:
        zext_ref[:, HALO - POOL_HIST:HALO, :] = hist_ref[...]
    else:
        @pl.when(j == 0)
        def _():
            zext_ref[:, 0:HALO, :] = jnp.zeros((s_t, HALO, WIDTH_B), F32)

        @pl.when(j > 0)
        def _():
            zext_ref[:, 0:HALO, :] = carry_ref[...]

    for r0 in range(0, l_t, sub):
        x = x_ref[:, r0:r0 + sub, :].reshape(ts, D_MODEL)
        h = _rmsnorm(x, gmix_ref[...]).astype(BF16)
        proj = _dot(h, win_ref[...])
        ua = _gelu_tanh(proj[:, :2 * WIDTH_A])
        u = ua[:, :WIDTH_A]
        v = _rmsnorm(ua[:, WIDTH_A:], gv_ref[...])
        z3 = proj[:, 2 * WIDTH_A:].reshape(s_t, sub, WIDTH_B)

        if has_hist:
            mixed = _gate_short(v.reshape(s_t, sub, WIDTH_A), ws_ref, sub) + mb_ref[...][None, :, :]
        else:
            mixed = _gate_chunks(v, ws_ref, ts).reshape(ts // CHUNK, CHUNK, WIDTH_A) + mb_ref[...][None, :, :]
        a_out = u * mixed.reshape(ts, WIDTH_A)

        zext_ref[:, HALO + r0:HALO + r0 + sub, :] = z3
        sums = _trailing_sums(zext_ref[:, r0:r0 + HALO + sub, :], sub)
        pos = pos0 + j * l_t + r0 + lax.broadcasted_iota(jnp.int32, (1, sub, 1), 1)
        pooled_out = []
        for g, w in enumerate(POOL_WINDOWS):
            sl = slice(g * POOL_GROUP_DIM, (g + 1) * POOL_GROUP_DIM)
            inv_cnt = 1.0 / jnp.minimum(w, pos + 1).astype(F32)
            pooled = sums[g] * inv_cnt - z3[:, :, sl]
            pooled_out.append(_dot(pooled.reshape(ts, POOL_GROUP_DIM).astype(BF16), wpool_ref[g]))
        b_out = jnp.concatenate(pooled_out, axis=-1) * spool_ref[...]

        cat = jnp.concatenate([a_out, b_out], axis=-1).astype(BF16)
        x1 = x + _dot(cat, wout_ref[...])
        x1_ref[:, r0:r0 + sub, :] = x1.reshape(s_t, sub, D_MODEL)
        if has_hist:
            v_ref[...] = v.reshape(s_t, sub, WIDTH_A)
        elif r0 + sub == l_t:
            v_ref[...] = v[ts - CHUNK:, :].reshape(1, CHUNK, WIDTH_A)

    zt_ref[...] = zext_ref[:, l_t + 1:l_t + HALO, :]
    if not has_hist:
        carry_ref[...] = zext_ref[:, l_t:l_t + HALO, :]


def _mix(x3, hist, layer, wts, *, s_t, l_t, pos0):
    s_all, l_all, _ = x3.shape
    has_hist = hist is not None
    grid = (s_all // s_t, l_all // l_t)
    x_spec = pl.BlockSpec((s_t, l_t, D_MODEL), lambda b, j: (b, j, 0))
    in_specs = [x_spec]
    args = [x3]
    if has_hist:
        in_specs.append(pl.BlockSpec((None, s_t, POOL_HIST, WIDTH_B), lambda b, j: (layer, b, 0, 0)))
        args.append(hist)
    gate_w = ("coef_s", "bias_s") if has_hist else ("w_s", "bias_p")
    for name in ("g_mix", "w_in", "g_v") + gate_w + ("w_pool", "s_pool", "w_out"):
        in_specs.append(_layer_spec(wts[name], layer))
        args.append(wts[name])
    if has_hist:
        v_shape, v_blk = (s_all, l_all, WIDTH_A), (s_t, l_t, WIDTH_A)
    else:
        v_shape, v_blk = (s_all, CHUNK, WIDTH_A), (1, CHUNK, WIDTH_A)
    out_shape = [jax.ShapeDtypeStruct(x3.shape, F32),
                 jax.ShapeDtypeStruct(v_shape, F32),
                 jax.ShapeDtypeStruct((s_all, POOL_HIST, WIDTH_B), F32)]
    out_specs = [x_spec,
                 pl.BlockSpec(v_blk, lambda b, j: (b, 0, 0)),
                 pl.BlockSpec((s_t, POOL_HIST, WIDTH_B), lambda b, j: (b, 0, 0))]
    scratch = [pltpu.VMEM((s_t, HALO + l_t, WIDTH_B), F32)]
    if not has_hist:
        scratch.append(pltpu.VMEM((s_t, HALO, WIDTH_B), F32))
    return pl.pallas_call(
        functools.partial(_mix_kernel, s_t=s_t, l_t=l_t, pos0=pos0, has_hist=has_hist),
        grid=grid, in_specs=in_specs, out_specs=out_specs, out_shape=out_shape,
        scratch_shapes=scratch, compiler_params=_PARAMS,
        name="mix_sample" if has_hist else "mix_prompt",
    )(*args)


def _softmax_rows(sc):
    m = jnp.max(sc, axis=-1, keepdims=True)
    e = jnp.exp(sc - m)
    return e * (1.0 / jnp.sum(e, axis=-1, keepdims=True))


def _attend_split(qs, k_ref, v_ref, s):
    heads = []
    for hh in range(X_HEADS):
        sl = slice(hh * X_HEAD_DIM, (hh + 1) * X_HEAD_DIM)
        p = _softmax_rows(_dot_nt(qs[:, sl].astype(BF16), k_ref[s, :, sl]))
        heads.append(_dot(p.astype(BF16), v_ref[s, :, sl]))
    return jnp.concatenate(heads, axis=-1)


def _attend_native(qs, k_ref, v_ref, s, l_t):
    q_rows = jnp.concatenate([qs[:, hh * X_HEAD_DIM:(hh + 1) * X_HEAD_DIM] for hh in range(X_HEADS)], axis=0)
    k2 = k_ref[0, s].reshape(N_MEM * X_HEADS, X_HEAD_DIM).astype(BF16)
    v2 = v_ref[0, s].reshape(N_MEM * X_HEADS, X_HEAD_DIM).astype(BF16)
    sc = _dot_nt(q_rows.astype(BF16), k2)
    r_i = lax.broadcasted_iota(jnp.int32, sc.shape, 0)
    c_i = lax.broadcasted_iota(jnp.int32, sc.shape, 1)
    same_head = (r_i // l_t) == (c_i % X_HEADS)
    p = _softmax_rows(jnp.where(same_head, sc, -jnp.inf))
    o_rows = _dot(p.astype(BF16), v2)
    return jnp.concatenate([o_rows[hh * l_t:(hh + 1) * l_t] for hh in range(X_HEADS)], axis=-1)


def _xattn_kernel(xp_ref, xs_ref, kp_ref, vp_ref, ks_ref, vs_ref, g_ref, wq_ref, wo_ref, eg_ref, eu_ref, ed_ref,
                  op_ref, os_ref, eg_o, eu_o, ed_o, *, l_p, s_s, l_s):
    eg_o[...] = eg_ref[...].astype(BF16)
    eu_o[...] = eu_ref[...].astype(BF16)
    ed_o[...] = ed_ref[...].astype(BF16)
    n_s = s_s * l_s
    x = jnp.concatenate([xp_ref[...].reshape(l_p, D_MODEL), xs_ref[...].reshape(n_s, D_MODEL)], axis=0)
    h = _rmsnorm(x, g_ref[...]).astype(BF16)
    q = _dot(h, wq_ref[...]) * (X_HEAD_DIM ** -0.5)
    outs = [_attend_split(q[:l_p], kp_ref, vp_ref, 0)]
    for s in range(s_s):
        outs.append(_attend_native(q[l_p + s * l_s:l_p + (s + 1) * l_s], ks_ref, vs_ref, s, l_s))
    o = jnp.concatenate(outs, axis=0)
    x2 = x + _dot(o.astype(BF16), wo_ref[...])
    op_ref[...] = x2[:l_p].reshape(1, l_p, D_MODEL)
    os_ref[...] = x2[l_p:].reshape(s_s, l_s, D_MODEL)


def _xattn(xp, xs, mem_kb, mem_vb, cache_k, cache_v, layer, wts, *, l_p):
    bp, seq, _ = xp.shape
    s_all, l_s, _ = xs.shape
    tiles = seq // l_p
    steps = bp * tiles
    s_s = s_all // steps
    parts = steps // N_EXPERTS
    assert s_s * steps == s_all and seq % l_p == 0 and parts * N_EXPERTS == steps
    rows_in, rows_mid = D_MODEL // parts, D_EXPERT // parts

    def step(b, j):
        return b * tiles + j

    xp_spec = pl.BlockSpec((1, l_p, D_MODEL), lambda b, j: (b, j, 0))
    xs_spec = pl.BlockSpec((s_s, l_s, D_MODEL), lambda b, j: (step(b, j), 0, 0))
    mp_spec = pl.BlockSpec((1, N_MEM, D_MODEL), lambda b, j: (layer * bp + b, 0, 0))
    ms_spec = pl.BlockSpec((1, s_s, N_MEM, X_HEADS, X_HEAD_DIM), lambda b, j: (layer, step(b, j), 0, 0, 0))
    gu_in = pl.BlockSpec((None, None, rows_in, D_EXPERT),
                         lambda b, j: (layer, step(b, j) // parts, step(b, j) % parts, 0))
    dn_in = pl.BlockSpec((None, None, rows_mid, D_MODEL),
                         lambda b, j: (layer, step(b, j) // parts, step(b, j) % parts, 0))
    gu_out = pl.BlockSpec((None, rows_in, D_EXPERT),
                          lambda b, j: (step(b, j) // parts // PER_GROUP, step(b, j) % parts,
                                        step(b, j) // parts % PER_GROUP))
    dn_out = pl.BlockSpec((None, rows_mid, D_MODEL),
                          lambda b, j: (step(b, j) // parts // PER_GROUP,
                                        (step(b, j) // parts % PER_GROUP) * parts + step(b, j) % parts, 0))
    names = ("g_xattn", "w_xq", "w_xo")
    gu_sds = jax.ShapeDtypeStruct((N_GROUPS, D_MODEL, PER_GROUP * D_EXPERT), BF16)
    dn_sds = jax.ShapeDtypeStruct((N_GROUPS, PER_GROUP * D_EXPERT, D_MODEL), BF16)
    return pl.pallas_call(
        functools.partial(_xattn_kernel, l_p=l_p, s_s=s_s, l_s=l_s),
        grid=(bp, tiles),
        in_specs=([xp_spec, xs_spec, mp_spec, mp_spec, ms_spec, ms_spec] + [_layer_spec(wts[n], layer) for n in names]
                  + [gu_in, gu_in, dn_in]),
        out_specs=[xp_spec, xs_spec, gu_out, gu_out, dn_out],
        out_shape=[jax.ShapeDtypeStruct(xp.shape, F32), jax.ShapeDtypeStruct(xs.shape, F32), gu_sds, gu_sds, dn_sds],
        compiler_params=_PARAMS,
        name="xattn",
    )(xp, xs, mem_kb, mem_vb, cache_k, cache_v, *[wts[n] for n in names],
      wts["w_gate"], wts["w_up"], wts["w_down"])


def _first_index(vals, target):
    idx = jnp.full(target.shape, len(vals) - 1, jnp.int32)
    for i in range(len(vals) - 2, -1, -1):
        idx = jnp.where(vals[i] == target, i, idx)
    return idx


def _max_of(vals):
    m = vals[0]
    for v in vals[1:]:
        m = jnp.maximum(m, v)
    return m


def _route(lt):
    rows = [lt[i:i + 1, :] for i in range(N_GROUPS + N_EXPERTS)]
    g_l = rows[:N_GROUPS]
    gmax = _max_of(g_l)
    gsum = g_l[0] * 0.0
    for v in g_l:
        gsum = gsum + jnp.exp(v - gmax)
    g_p = 1.0 / gsum
    g_idx = _first_index(g_l, gmax)
    e_sel = []
    for j in range(PER_GROUP):
        v = rows[N_GROUPS + PER_GROUP * (N_GROUPS - 1) + j]
        for g in range(N_GROUPS - 2, -1, -1):
            v = jnp.where(g_idx == g, rows[N_GROUPS + PER_GROUP * g + j], v)
        e_sel.append(v)
    v1 = _max_of(e_sel)
    i1 = _first_index(e_sel, v1)
    neg = jnp.float32(-jnp.inf)
    e_rest = [jnp.where(i1 == j, neg, e_sel[j]) for j in range(PER_GROUP)]
    v2 = _max_of(e_rest)
    i2 = _first_index(e_rest, v2)
    t = jnp.exp(v2 - v1)
    w1 = g_p / (1.0 + t)
    w2 = g_p * t / (1.0 + t)
    gates = [jnp.where(i1 == j, w1, 0.0) + jnp.where(i2 == j, w2, 0.0) for j in range(PER_GROUP)]
    return g_idx, gates


def _stack_rows(rows, n_rows, tm):
    ri = lax.broadcasted_iota(jnp.int32, (n_rows, tm), 0)
    out = jnp.zeros((n_rows, tm), F32)
    for i, r in enumerate(rows):
        out = jnp.where(ri == i, r, out)
    return out


def _moe_plan(x, g_ref, wr_ref, br_ref):
    tm = MOE_TM
    h = _rmsnorm(x, g_ref[...]).astype(BF16)
    logits = _dot(h, wr_ref[...]) + br_ref[...]
    g_idx, gates = _route(logits.T)

    member = [(g_idx == g).astype(F32) for g in range(N_GROUPS)]
    t_r = lax.broadcasted_iota(jnp.int32, (tm, tm), 0)
    t_c = lax.broadcasted_iota(jnp.int32, (tm, tm), 1)
    upper = jnp.where(t_r <= t_c, 1.0, 0.0).astype(BF16)
    incl = _dot(_stack_rows(member, GATE_ROWS, tm).astype(BF16), upper)
    end_blk = []
    posf = jnp.zeros((1, tm), F32)
    run = jnp.zeros((1, 1), F32)
    for g in range(N_GROUPS):
        n_g = jnp.sum(member[g], axis=-1, keepdims=True)
        posf = posf + member[g] * (run * MOE_BM + incl[g:g + 1, :] - 1.0)
        run = run + jnp.floor((n_g + (MOE_BM - 1)) * (1.0 / MOE_BM))
        end_blk.append(run)

    g_hi = [gt.astype(BF16).astype(F32) for gt in gates]
    g_lo = [gt - hi for gt, hi in zip(gates, g_hi)]
    gmat = _stack_rows(g_hi + g_lo, GATE_ROWS, tm).astype(BF16)
    return h, posf, gmat, end_blk


def _moe_kernel(x_ref, xn_ref, g_ref, wr_ref, br_ref, wg_ref, wu_ref, wd_ref, gf_ref, o_ref,
                ys_ref, h_ref, pos_ref, gmat_ref, cnt_ref, *, final):
    tm = MOE_TM

    def park_plan(x):
        h, posf, gmat, end_blk = _moe_plan(x, g_ref, wr_ref, br_ref)
        h_ref[...] = h
        pos_ref[...] = jnp.broadcast_to(posf, pos_ref.shape)
        gmat_ref[...] = gmat
        for k, e in enumerate(end_blk):
            cnt_ref[k] = e.astype(jnp.int32)[0, 0]

    @pl.when(pl.program_id(0) == 0)
    def _():
        park_plan(x_ref[...])

    posf = pos_ref[0:1, :]
    pos = posf.astype(jnp.int32)
    gmat = gmat_ref[...]
    e0, e1, e2, n_blk = [cnt_ref[k] for k in range(N_GROUPS)]

    r_iota = lax.broadcasted_iota(jnp.int32, (MOE_BM, tm), 0)

    def block(b):
        r0 = pl.multiple_of(b * MOE_BM, MOE_BM)
        grp = (b >= e0).astype(jnp.int32) + (b >= e1).astype(jnp.int32) + (b >= e2).astype(jnp.int32)
        pb = jnp.where(r_iota + r0 == pos, 1.0, 0.0).astype(BF16)
        xb = _dot(pb, h_ref[...]).astype(BF16)
        gs = _dot_nt(pb, gmat)
        hg = _dot(xb, wg_ref[grp])
        hu = _dot(xb, wu_ref[grp])
        gate_cols = jnp.concatenate(
            [jnp.broadcast_to(gs[:, j:j + 1] + gs[:, PER_GROUP + j:PER_GROUP + j + 1], (MOE_BM, D_EXPERT))
             for j in range(PER_GROUP)], axis=-1)
        a = (_silu(hg) * hu * gate_cols).astype(BF16)
        ys_ref[pl.ds(r0, MOE_BM), :] = _dot(a, wd_ref[grp]).astype(BF16)

    n_iter = lax.shift_right_logical(n_blk + (MOE_UNROLL - 1), MOE_UNROLL.bit_length() - 1)

    def blocks(i, carry):
        for u in range(MOE_UNROLL):
            block(i * MOE_UNROLL + u)
        return carry

    lax.fori_loop(0, n_iter, blocks, 0)

    def clear(b, carry):
        r0 = pl.multiple_of(b * MOE_BM, MOE_BM)
        ys_ref[pl.ds(r0, MOE_BM), :] = jnp.zeros((MOE_BM, D_MODEL), BF16)
        return carry

    lax.fori_loop(n_iter * MOE_UNROLL, MOE_R // MOE_BM, clear, 0)

    pos_col = jnp.broadcast_to(posf, (LANES, tm)).T
    lane = lax.broadcasted_iota(jnp.int32, (tm, LANES), 1).astype(F32)
    pt = jnp.concatenate(
        [jnp.where(pos_col == lane + float(LANES * k), 1.0, 0.0).astype(BF16) for k in range(MOE_R // LANES)],
        axis=-1)
    y = x_ref[...] + _dot(pt, ys_ref[...])
    if final:
        y = _rmsnorm(y, gf_ref[...])
    o_ref[...] = y
    park_plan(xn_ref[...])


def _moe(x2, layer, wts, experts, g_final, *, final):
    n_tiles = x2.shape[0] // MOE_TM
    assert n_tiles * MOE_TM == x2.shape[0] and MOE_R % (MOE_BM * MOE_UNROLL) == 0
    assert MOE_TM + N_GROUPS * (MOE_BM - 1) <= MOE_R
    x_spec = pl.BlockSpec((MOE_TM, D_MODEL), lambda i: (i, 0))
    next_spec = pl.BlockSpec((MOE_TM, D_MODEL), lambda i: (jnp.minimum(i + 1, n_tiles - 1), 0))
    names = ("g_ffn", "w_r", "b_r")
    return pl.pallas_call(
        functools.partial(_moe_kernel, final=final),
        grid=(n_tiles,),
        in_specs=([x_spec, next_spec] + [_layer_spec(wts[n], layer, pipeline_mode=pl.Buffered(1)) for n in names]
                  + [pl.BlockSpec(w.shape, lambda i: (0, 0, 0), pipeline_mode=pl.Buffered(1)) for w in experts]
                  + [pl.BlockSpec(g_final.shape, lambda i: (0, 0))]),
        out_specs=x_spec,
        out_shape=jax.ShapeDtypeStruct(x2.shape, F32),
        scratch_shapes=[pltpu.VMEM((MOE_R, D_MODEL), BF16),
                        pltpu.VMEM((MOE_TM, D_MODEL), BF16),
                        pltpu.VMEM((8, MOE_TM), F32),
                        pltpu.VMEM((GATE_ROWS, MOE_TM), BF16),
                        pltpu.SMEM((N_GROUPS,), jnp.int32)],
        compiler_params=pltpu.CompilerParams(dimension_semantics=("arbitrary",), vmem_limit_bytes=VMEM_LIMIT_BYTES),
        name="moe",
    )(x2, x2, *[wts[n] for n in names], *experts, g_final)


def kernel(x_prompt, x_sample, cache_mem_k, cache_mem_v, state_pool, mem_prompt, g_mix, w_in, g_v, w_s, b_s, w_pool, s_pool, w_out, g_mem, g_xattn, w_xq, w_xk, w_xv, w_xo, g_ffn, w_group, b_group, w_router, b_router, w_gate, w_up, w_down, g_final):
    bp, seq, _ = x_prompt.shape
    bs, dseq, _ = x_sample.shape

    mem_k, mem_v, mem_kb, mem_vb, w_in_b, w_out_b, w_xq_b, w_xo_b = _memkv(
        mem_prompt.reshape(bp * N_MEM, D_MODEL), g_mem.reshape(1, D_MODEL), w_xk, w_xv, w_in, w_out, w_xq, w_xo)
    mem_kb = mem_kb.reshape(DEPTH * bp, N_MEM, D_MODEL)
    mem_vb = mem_vb.reshape(DEPTH * bp, N_MEM, D_MODEL)

    pad_r = ROUTER_LANES - N_GROUPS - N_EXPERTS
    bias_full = jnp.repeat(jnp.swapaxes(b_s, 1, 2), HEAD_DIM_A, axis=2)
    wts = {
        "g_mix": g_mix.reshape(DEPTH, 1, D_MODEL), "w_in": w_in_b, "g_v": g_v.reshape(DEPTH, 1, WIDTH_A),
        "w_s": w_s.astype(BF16), "bias_p": bias_full,
        "coef_s": jnp.repeat(jnp.transpose(w_s[:, :, :dseq, :dseq], (0, 3, 2, 1)), HEAD_DIM_A, axis=3),
        "bias_s": bias_full[:, :dseq, :],
        "w_pool": w_pool.astype(BF16), "s_pool": s_pool.reshape(DEPTH, 1, WIDTH_B), "w_out": w_out_b,
        "g_xattn": g_xattn.reshape(DEPTH, 1, D_MODEL), "w_xq": w_xq_b, "w_xo": w_xo_b,
        "g_ffn": g_ffn.reshape(DEPTH, 1, D_MODEL),
        "w_r": jnp.concatenate([w_group, w_router, jnp.zeros((DEPTH, D_MODEL, pad_r), F32)], axis=-1).astype(BF16),
        "b_r": jnp.concatenate([b_group, b_router, jnp.zeros((DEPTH, pad_r), F32)], axis=-1).reshape(DEPTH, 1, ROUTER_LANES),
        "w_gate": w_gate, "w_up": w_up, "w_down": w_down,
    }
    g_final2 = g_final.reshape(1, D_MODEL)

    yp, ys = x_prompt, x_sample
    cv_p, cv_s, pl_p, pl_s = [], [], [], []
    for l in range(DEPTH):
        final = l == DEPTH - 1

        yp, cvp, plp = _mix(yp, None, l, wts, s_t=1, l_t=MIX_TM, pos0=0)
        ys, cvs, pls = _mix(ys, state_pool, l, wts, s_t=SAMPLE_MIX_SEQS, l_t=dseq, pos0=PAST_LEN)
        yp, ys, *experts = _xattn(yp, ys, mem_kb, mem_vb, cache_mem_k, cache_mem_v, l, wts, l_p=PROMPT_TM)
        yp = _moe(yp.reshape(bp * seq, D_MODEL), l, wts, experts, g_final2, final=final).reshape(bp, seq, D_MODEL)
        ys = _moe(ys.reshape(bs * dseq, D_MODEL), l, wts, experts, g_final2, final=final).reshape(bs, dseq, D_MODEL)

        cv_p.append(cvp.reshape(bp, CHUNK, HEADS_A, HEAD_DIM_A))
        cv_s.append(cvs.reshape(bs, dseq, HEADS_A, HEAD_DIM_A))
        pl_p.append(plp)
        pl_s.append(pls)

    return (yp, ys, mem_k, mem_v, jnp.stack(cv_p), jnp.stack(cv_s), jnp.stack(pl_p), jnp.stack(pl_s))
```

```python
import functools

import jax
import jax.numpy as jnp
from jax import lax
from jax.experimental import pallas as pl
from jax.experimental.pallas import tpu as pltpu

D_MODEL = 1024
DEPTH = 4
PAST_LEN = 16384
CHUNK = 128
WIDTH_A = 512
WIDTH_B = 512
HEADS_A = 8
HEAD_DIM_A = 64
POOL_WINDOWS = (2, 4, 8, 16)
POOL_GROUP_DIM = 128
POOL_HIST = 15
HALO = 16
PROJ_WIDTH = 2 * WIDTH_A + WIDTH_B
N_MEM = 256
X_HEADS = 4
X_HEAD_DIM = 256
N_GROUPS = 4
PER_GROUP = 4
N_EXPERTS = 16
D_EXPERT = 256
EPS = 1e-6
ROUTER_LANES = 128
LANES = 128
GATE_ROWS = 16

MEMKV_SEQS = 2
PROMPT_TM = 512
MIX_TM = 1024
MIX_SUB = 256
SAMPLE_MIX_SEQS = 32
SAMPLE_ATT_SEQS = 8
MOE_TM = 512
MOE_BM = 128
MOE_R = 1024
MOE_R_COMMON = 768
MOE_UNROLL = 2

F32 = jnp.float32
BF16 = jnp.bfloat16

VMEM_LIMIT_BYTES = 56 * 1024 * 1024


def _rmsnorm(x, g):
    r = lax.rsqrt(jnp.mean(x * x, axis=-1, keepdims=True) + EPS)
    return x * r * g


def _gelu_tanh(x):
    c = 0.7978845608028654
    return 0.5 * x * (1.0 + jnp.tanh(c * (x + 0.044715 * (x * x * x))))


def _silu(x):
    return x * (1.0 / (1.0 + jnp.exp(-x)))


def _dot(a, b):
    return jnp.dot(a, b, preferred_element_type=F32)


def _dot_nt(a, b):
    return lax.dot_general(a, b, (((1,), (1,)), ((), ())), preferred_element_type=F32)


def _layer_spec(arr, layer, **kw):
    nd = arr.ndim - 1
    return pl.BlockSpec((None,) + arr.shape[1:], lambda *_: (layer,) + (0,) * nd, **kw)


_PARAMS = pltpu.CompilerParams(dimension_semantics=("arbitrary", "arbitrary"), vmem_limit_bytes=VMEM_LIMIT_BYTES)


def _memkv_kernel(mem_ref, g_ref, wk_ref, wv_ref, win_ref, wout_ref, wq_ref, wo_ref,
                  k_ref, v_ref, kb_ref, vb_ref, win_o, wout_o, wq_o, wo_o, wkb_ref, wvb_ref):
    win_o[...] = win_ref[...].astype(BF16)
    wout_o[...] = wout_ref[...].astype(BF16)
    wq_o[...] = wq_ref[...].astype(BF16)
    wo_o[...] = wo_ref[...].astype(BF16)

    @pl.when(pl.program_id(1) == 0)
    def _():
        wkb_ref[...] = wk_ref[0].astype(BF16)
        wvb_ref[...] = wv_ref[0].astype(BF16)

    h = _rmsnorm(mem_ref[...], g_ref[...]).astype(BF16)
    k = _dot(h, wkb_ref[...])
    v = _dot(h, wvb_ref[...])
    for s in range(MEMKV_SEQS):
        rows = slice(s * N_MEM, (s + 1) * N_MEM)
        for hh in range(X_HEADS):
            sl = slice(hh * X_HEAD_DIM, (hh + 1) * X_HEAD_DIM)
            k_ref[0, s, :, hh, :] = k[rows, sl]
            v_ref[0, s, :, hh, :] = v[rows, sl]
    kb_ref[0] = k.astype(BF16)
    vb_ref[0] = v.astype(BF16)


def _memkv(mem2d, g_mem, wk, wv, w_in, w_out, w_xq, w_xo):
    rows = mem2d.shape[0]
    tile = MEMKV_SEQS * N_MEM
    n_r = rows // tile
    slab = D_MODEL // n_r
    assert n_r * tile == rows and slab * n_r == D_MODEL
    out_sds = jax.ShapeDtypeStruct((DEPTH, rows // N_MEM, N_MEM, X_HEADS, X_HEAD_DIM), F32)
    outb_sds = jax.ShapeDtypeStruct((DEPTH, rows, D_MODEL), BF16)
    w_spec = pl.BlockSpec((1, D_MODEL, D_MODEL), lambda l, r: (l, 0, 0))
    o5_spec = pl.BlockSpec((1, MEMKV_SEQS, N_MEM, X_HEADS, X_HEAD_DIM), lambda l, r: (l, r, 0, 0, 0))
    o_spec = pl.BlockSpec((1, tile, D_MODEL), lambda l, r: (l, r, 0))
    dense = (w_in, w_out, w_xq, w_xo)
    slab_specs = [pl.BlockSpec((1, slab, w.shape[2]), lambda l, r: (l, r, 0)) for w in dense]
    return pl.pallas_call(
        _memkv_kernel,
        grid=(DEPTH, n_r),
        in_specs=[pl.BlockSpec((tile, D_MODEL), lambda l, r: (r, 0)),
                  pl.BlockSpec((1, D_MODEL), lambda l, r: (0, 0)),
                  w_spec, w_spec] + slab_specs,
        out_specs=[o5_spec, o5_spec, o_spec, o_spec] + slab_specs,
        out_shape=[out_sds, out_sds, outb_sds, outb_sds] + [jax.ShapeDtypeStruct(w.shape, BF16) for w in dense],
        scratch_shapes=[pltpu.VMEM((D_MODEL, D_MODEL), BF16), pltpu.VMEM((D_MODEL, D_MODEL), BF16)],
        compiler_params=_PARAMS,
        name="memkv",
    )(mem2d, g_mem, wk, wv, *dense)


def _gate_chunks(v, ws_ref, tm):
    r_i = lax.broadcasted_iota(jnp.int32, (CHUNK, CHUNK), 0)
    c_i = lax.broadcasted_iota(jnp.int32, (CHUNK, CHUNK), 1)
    causal = c_i <= r_i
    lane = lax.broadcasted_iota(jnp.int32, (CHUNK, LANES), 1)
    zero_b = jnp.zeros((), BF16)
    vb = v.astype(BF16)
    pair_w = []
    for p in range(HEADS_A // 2):
        m_lo = jnp.where(causal, ws_ref[2 * p], zero_b)
        m_hi = jnp.where(causal, ws_ref[2 * p + 1], zero_b)
        pair_w.append(jnp.concatenate([m_lo, m_hi], axis=1))
    rows = []
    for c in range(tm // CHUNK):
        cols = []
        for p in range(HEADS_A // 2):
            vp = vb[c * CHUNK:(c + 1) * CHUNK, LANES * p:LANES * (p + 1)]
            lo = jnp.where(lane < HEAD_DIM_A, vp, zero_b)
            hi = jnp.where(lane >= HEAD_DIM_A, vp, zero_b)
            cols.append(_dot(pair_w[p], jnp.concatenate([lo, hi], axis=0)))
        rows.append(jnp.concatenate(cols, axis=-1))
    return jnp.concatenate(rows, axis=0)


def _gate_short(v3, coef_ref, l_t):
    i_i = lax.broadcasted_iota(jnp.int32, (l_t, WIDTH_A), 0)
    acc = None
    for jj in range(l_t):
        cj = jnp.where(i_i >= jj, coef_ref[jj], 0.0)
        term = cj[None, :, :] * v3[:, jj:jj + 1, :]
        acc = term if acc is None else acc + term
    return acc


def _trailing_sums(zx, n_new):
    outs = []
    for g, w in enumerate(POOL_WINDOWS):
        a = zx[:, :, g * POOL_GROUP_DIM:(g + 1) * POOL_GROUP_DIM]
        span = 1
        while span < w:
            n = a.shape[1]
            a = a[:, span:, :] + a[:, :n - span, :]
            span *= 2
        first = HALO + 1 - w
        outs.append(a[:, first:first + n_new, :])
    return outs


def _mix_kernel(*refs, s_t, l_t, pos0, has_hist):
    if has_hist:
        (x_ref, hist_ref, gmix_ref, win_ref, gv_ref, ws_ref, mb_ref, wpool_ref, spool_ref, wout_ref,
         x1_ref, v_ref, zt_ref, zext_ref) = refs
    else:
        (x_ref, gmix_ref, win_ref, gv_ref, ws_ref, mb_ref, wpool_ref, spool_ref, wout_ref,
         x1_ref, v_ref, zt_ref, zext_ref, carry_ref) = refs
    j = pl.program_id(1)
    sub = min(l_t, MIX_SUB)
    ts = s_t * sub

    if has_hist:
        zext_ref[:, HALO - POOL_HIST:HALO, :] = hist_ref[...]
    else:
        @pl.when(j == 0)
        def _():
            zext_ref[:, 0:HALO, :] = jnp.zeros((s_t, HALO, WIDTH_B), F32)

        @pl.when(j > 0)
        def _():
            zext_ref[:, 0:HALO, :] = carry_ref[...]

    for r0 in range(0, l_t, sub):
        x = x_ref[:, r0:r0 + sub, :].reshape(ts, D_MODEL)
        h = _rmsnorm(x, gmix_ref[...]).astype(BF16)
        proj = _dot(h, win_ref[...])
        ua = _gelu_tanh(proj[:, :2 * WIDTH_A])
        u = ua[:, :WIDTH_A]
        v = _rmsnorm(ua[:, WIDTH_A:], gv_ref[...])
        z3 = proj[:, 2 * WIDTH_A:].reshape(s_t, sub, WIDTH_B)

        if has_hist:
            mixed = _gate_short(v.reshape(s_t, sub, WIDTH_A), ws_ref, sub) + mb_ref[...][None, :, :]
        else:
            mixed = _gate_chunks(v, ws_ref, ts).reshape(ts // CHUNK, CHUNK, WIDTH_A) + mb_ref[...][None, :, :]
        a_out = u * mixed.reshape(ts, WIDTH_A)

        zext_ref[:, HALO + r0:HALO + r0 + sub, :] = z3
        sums = _trailing_sums(zext_ref[:, r0:r0 + HALO + sub, :], sub)
        pos = pos0 + j * l_t + r0 + lax.broadcasted_iota(jnp.int32, (1, sub, 1), 1)
        pooled_out = []
        for g, w in enumerate(POOL_WINDOWS):
            sl = slice(g * POOL_GROUP_DIM, (g + 1) * POOL_GROUP_DIM)
            inv_cnt = 1.0 / jnp.minimum(w, pos + 1).astype(F32)
            pooled = sums[g] * inv_cnt - z3[:, :, sl]
            pooled_out.append(_dot(pooled.reshape(ts, POOL_GROUP_DIM).astype(BF16), wpool_ref[g]))
        b_out = jnp.concatenate(pooled_out, axis=-1) * spool_ref[...]

        cat = jnp.concatenate([a_out, b_out], axis=-1).astype(BF16)
        x1 = x + _dot(cat, wout_ref[...])
        x1_ref[:, r0:r0 + sub, :] = x1.reshape(s_t, sub, D_MODEL)
        if has_hist:
            v_ref[...] = v.reshape(s_t, sub, WIDTH_A)
        elif r0 + sub == l_t:
            v_ref[...] = v[ts - CHUNK:, :].reshape(1, CHUNK, WIDTH_A)

    zt_ref[...] = zext_ref[:, l_t + 1:l_t + HALO, :]
    if not has_hist:
        carry_ref[...] = zext_ref[:, l_t:l_t + HALO, :]


def _mix(x3, hist, layer, wts, *, s_t, l_t, pos0):
    s_all, l_all, _ = x3.shape
    has_hist = hist is not None
    grid = (s_all // s_t, l_all // l_t)
    x_spec = pl.BlockSpec((s_t, l_t, D_MODEL), lambda b, j: (b, j, 0))
    in_specs = [x_spec]
    args = [x3]
    if has_hist:
        in_specs.append(pl.BlockSpec((None, s_t, POOL_HIST, WIDTH_B), lambda b, j: (layer, b, 0, 0)))
        args.append(hist)
    gate_w = ("coef_s", "bias_s") if has_hist else ("w_s", "bias_p")
    for name in ("g_mix", "w_in", "g_v") + gate_w + ("w_pool", "s_pool", "w_out"):
        in_specs.append(_layer_spec(wts[name], layer))
        args.append(wts[name])
    if has_hist:
        v_shape, v_blk = (s_all, l_all, WIDTH_A), (s_t, l_t, WIDTH_A)
    else:
        v_shape, v_blk = (s_all, CHUNK, WIDTH_A), (1, CHUNK, WIDTH_A)
    out_shape = [jax.ShapeDtypeStruct(x3.shape, F32),
                 jax.ShapeDtypeStruct(v_shape, F32),
                 jax.ShapeDtypeStruct((s_all, POOL_HIST, WIDTH_B), F32)]
    out_specs = [x_spec,
                 pl.BlockSpec(v_blk, lambda b, j: (b, 0, 0)),
                 pl.BlockSpec((s_t, POOL_HIST, WIDTH_B), lambda b, j: (b, 0, 0))]
    scratch = [pltpu.VMEM((s_t, HALO + l_t, WIDTH_B), F32)]
    if not has_hist:
        scratch.append(pltpu.VMEM((s_t, HALO, WIDTH_B), F32))
    return pl.pallas_call(
        functools.partial(_mix_kernel, s_t=s_t, l_t=l_t, pos0=pos0, has_hist=has_hist),
        grid=grid, in_specs=in_specs, out_specs=out_specs, out_shape=out_shape,
        scratch_shapes=scratch, compiler_params=_PARAMS,
        name="mix_sample" if has_hist else "mix_prompt",
    )(*args)


def _softmax_rows(sc):
    m = jnp.max(sc, axis=-1, keepdims=True)
    e = jnp.exp(sc - m)
    return e * (1.0 / jnp.sum(e, axis=-1, keepdims=True))


def _attend_split(qs, k_ref, v_ref, s):
    heads = []
    for hh in range(X_HEADS):
        sl = slice(hh * X_HEAD_DIM, (hh + 1) * X_HEAD_DIM)
        p = _softmax_rows(_dot_nt(qs[:, sl].astype(BF16), k_ref[s, :, sl]))
        heads.append(_dot(p.astype(BF16), v_ref[s, :, sl]))
    return jnp.concatenate(heads, axis=-1)


def _attend_native(qs, k_ref, v_ref, s, l_t):
    q_rows = jnp.concatenate([qs[:, hh * X_HEAD_DIM:(hh + 1) * X_HEAD_DIM] for hh in range(X_HEADS)], axis=0)
    k2 = k_ref[0, s].reshape(N_MEM * X_HEADS, X_HEAD_DIM).astype(BF16)
    v2 = v_ref[0, s].reshape(N_MEM * X_HEADS, X_HEAD_DIM).astype(BF16)
    sc = _dot_nt(q_rows.astype(BF16), k2)
    r_i = lax.broadcasted_iota(jnp.int32, sc.shape, 0)
    c_i = lax.broadcasted_iota(jnp.int32, sc.shape, 1)
    same_head = (r_i // l_t) == (c_i % X_HEADS)
    p = _softmax_rows(jnp.where(same_head, sc, -jnp.inf))
    o_rows = _dot(p.astype(BF16), v2)
    return jnp.concatenate([o_rows[hh * l_t:(hh + 1) * l_t] for hh in range(X_HEADS)], axis=-1)


def _xattn_kernel(xp_ref, xs_ref, kp_ref, vp_ref, ks_ref, vs_ref, g_ref, wq_ref, wo_ref, eg_ref, eu_ref, ed_ref,
                  op_ref, os_ref, eg_o, eu_o, ed_o, *, l_p, s_s, l_s):
    eg_o[...] = eg_ref[...].astype(BF16)
    eu_o[...] = eu_ref[...].astype(BF16)
    ed_o[...] = ed_ref[...].astype(BF16)
    n_s = s_s * l_s
    x = jnp.concatenate([xp_ref[...].reshape(l_p, D_MODEL), xs_ref[...].reshape(n_s, D_MODEL)], axis=0)
    h = _rmsnorm(x, g_ref[...]).astype(BF16)
    q = _dot(h, wq_ref[...]) * (X_HEAD_DIM ** -0.5)
    outs = [_attend_split(q[:l_p], kp_ref, vp_ref, 0)]
    for s in range(s_s):
        outs.append(_attend_native(q[l_p + s * l_s:l_p + (s + 1) * l_s], ks_ref, vs_ref, s, l_s))
    o = jnp.concatenate(outs, axis=0)
    x2 = x + _dot(o.astype(BF16), wo_ref[...])
    op_ref[...] = x2[:l_p].reshape(1, l_p, D_MODEL)
    os_ref[...] = x2[l_p:].reshape(s_s, l_s, D_MODEL)


def _xattn(xp, xs, mem_kb, mem_vb, cache_k, cache_v, layer, wts, *, l_p):
    bp, seq, _ = xp.shape
    s_all, l_s, _ = xs.shape
    tiles = seq // l_p
    steps = bp * tiles
    s_s = s_all // steps
    parts = steps // N_EXPERTS
    assert s_s * steps == s_all and seq % l_p == 0 and parts * N_EXPERTS == steps
    rows_in, rows_mid = D_MODEL // parts, D_EXPERT // parts

    def step(b, j):
        return b * tiles + j

    xp_spec = pl.BlockSpec((1, l_p, D_MODEL), lambda b, j: (b, j, 0))
    xs_spec = pl.BlockSpec((s_s, l_s, D_MODEL), lambda b, j: (step(b, j), 0, 0))
    mp_spec = pl.BlockSpec((1, N_MEM, D_MODEL), lambda b, j: (layer * bp + b, 0, 0))
    ms_spec = pl.BlockSpec((1, s_s, N_MEM, X_HEADS, X_HEAD_DIM), lambda b, j: (layer, step(b, j), 0, 0, 0))
    gu_in = pl.BlockSpec((None, None, rows_in, D_EXPERT),
                         lambda b, j: (layer, step(b, j) // parts, step(b, j) % parts, 0))
    dn_in = pl.BlockSpec((None, None, rows_mid, D_MODEL),
                         lambda b, j: (layer, step(b, j) // parts, step(b, j) % parts, 0))
    gu_out = pl.BlockSpec((None, rows_in, D_EXPERT),
                          lambda b, j: (step(b, j) // parts // PER_GROUP, step(b, j) % parts,
                                        step(b, j) // parts % PER_GROUP))
    dn_out = pl.BlockSpec((None, rows_mid, D_MODEL),
                          lambda b, j: (step(b, j) // parts // PER_GROUP,
                                        (step(b, j) // parts % PER_GROUP) * parts + step(b, j) % parts, 0))
    names = ("g_xattn", "w_xq", "w_xo")
    gu_sds = jax.ShapeDtypeStruct((N_GROUPS, D_MODEL, PER_GROUP * D_EXPERT), BF16)
    dn_sds = jax.ShapeDtypeStruct((N_GROUPS, PER_GROUP * D_EXPERT, D_MODEL), BF16)
    return pl.pallas_call(
        functools.partial(_xattn_kernel, l_p=l_p, s_s=s_s, l_s=l_s),
        grid=(bp, tiles),
        in_specs=([xp_spec, xs_spec, mp_spec, mp_spec, ms_spec, ms_spec] + [_layer_spec(wts[n], layer) for n in names]
                  + [gu_in, gu_in, dn_in]),
        out_specs=[xp_spec, xs_spec, gu_out, gu_out, dn_out],
        out_shape=[jax.ShapeDtypeStruct(xp.shape, F32), jax.ShapeDtypeStruct(xs.shape, F32), gu_sds, gu_sds, dn_sds],
        compiler_params=_PARAMS,
        name="xattn",
    )(xp, xs, mem_kb, mem_vb, cache_k, cache_v, *[wts[n] for n in names],
      wts["w_gate"], wts["w_up"], wts["w_down"])


def _first_index(vals, target):
    idx = jnp.full(target.shape, len(vals) - 1, jnp.int32)
    for i in range(len(vals) - 2, -1, -1):
        idx = jnp.where(vals[i] == target, i, idx)
    return idx


def _max_of(vals):
    m = vals[0]
    for v in vals[1:]:
        m = jnp.maximum(m, v)
    return m


def _route(lt):
    rows = [lt[i:i + 1, :] for i in range(N_GROUPS + N_EXPERTS)]
    g_l = rows[:N_GROUPS]
    gmax = _max_of(g_l)
    gsum = g_l[0] * 0.0
    for v in g_l:
        gsum = gsum + jnp.exp(v - gmax)
    g_p = 1.0 / gsum
    g_idx = _first_index(g_l, gmax)
    e_sel = []
    for j in range(PER_GROUP):
        v = rows[N_GROUPS + PER_GROUP * (N_GROUPS - 1) + j]
        for g in range(N_GROUPS - 2, -1, -1):
            v = jnp.where(g_idx == g, rows[N_GROUPS + PER_GROUP * g + j], v)
        e_sel.append(v)
    v1 = _max_of(e_sel)
    i1 = _first_index(e_sel, v1)
    neg = jnp.float32(-jnp.inf)
    e_rest = [jnp.where(i1 == j, neg, e_sel[j]) for j in range(PER_GROUP)]
    v2 = _max_of(e_rest)
    i2 = _first_index(e_rest, v2)
    t = jnp.exp(v2 - v1)
    w1 = g_p / (1.0 + t)
    w2 = g_p * t / (1.0 + t)
    gates = [jnp.where(i1 == j, w1, 0.0) + jnp.where(i2 == j, w2, 0.0) for j in range(PER_GROUP)]
    return g_idx, gates


def _stack_rows(rows, n_rows, tm):
    ri = lax.broadcasted_iota(jnp.int32, (n_rows, tm), 0)
    out = jnp.zeros((n_rows, tm), F32)
    for i, r in enumerate(rows):
        out = jnp.where(ri == i, r, out)
    return out


def _moe_plan(x, g_ref, wr_ref, br_ref):
    tm = MOE_TM
    h = _rmsnorm(x, g_ref[...]).astype(BF16)
    logits = _dot(h, wr_ref[...]) + br_ref[...]
    g_idx, gates = _route(logits.T)

    member = [(g_idx == g).astype(F32) for g in range(N_GROUPS)]
    t_r = lax.broadcasted_iota(jnp.int32, (tm, tm), 0)
    t_c = lax.broadcasted_iota(jnp.int32, (tm, tm), 1)
    upper = jnp.where(t_r <= t_c, 1.0, 0.0).astype(BF16)
    incl = _dot(_stack_rows(member, GATE_ROWS, tm).astype(BF16), upper)
    end_blk = []
    posf = jnp.zeros((1, tm), F32)
    run = jnp.zeros((1, 1), F32)
    for g in range(N_GROUPS):
        n_g = jnp.sum(member[g], axis=-1, keepdims=True)
        posf = posf + member[g] * (run * MOE_BM + incl[g:g + 1, :] - 1.0)
        run = run + jnp.floor((n_g + (MOE_BM - 1)) * (1.0 / MOE_BM))
        end_blk.append(run)

    g_hi = [gt.astype(BF16).astype(F32) for gt in gates]
    g_lo = [gt - hi for gt, hi in zip(gates, g_hi)]
    gmat = _stack_rows(g_hi + g_lo, GATE_ROWS, tm).astype(BF16)
    return h, posf, gmat, end_blk


def _moe_kernel(x_ref, xn_ref, g_ref, wr_ref, br_ref, wg_ref, wu_ref, wd_ref, gf_ref, o_ref,
                ys_ref, h_ref, pos_ref, gmat_ref, cnt_ref, *, final):
    tm = MOE_TM

    def park_plan(x):
        h, posf, gmat, end_blk = _moe_plan(x, g_ref, wr_ref, br_ref)
        h_ref[...] = h
        pos_ref[...] = jnp.broadcast_to(posf, pos_ref.shape)
        gmat_ref[...] = gmat
        for k, e in enumerate(end_blk):
            cnt_ref[k] = e.astype(jnp.int32)[0, 0]

    @pl.when(pl.program_id(0) == 0)
    def _():
        park_plan(x_ref[...])

    posf = pos_ref[0:1, :]
    pos = posf.astype(jnp.int32)
    gmat = gmat_ref[...]
    e0, e1, e2, n_blk = [cnt_ref[k] for k in range(N_GROUPS)]

    r_iota = lax.broadcasted_iota(jnp.int32, (MOE_BM, tm), 0)

    def block(b):
        r0 = pl.multiple_of(b * MOE_BM, MOE_BM)
        grp = (b >= e0).astype(jnp.int32) + (b >= e1).astype(jnp.int32) + (b >= e2).astype(jnp.int32)
        pb = jnp.where(r_iota + r0 == pos, 1.0, 0.0).astype(BF16)
        xb = _dot(pb, h_ref[...]).astype(BF16)
        gs = _dot_nt(pb, gmat)
        hg = _dot(xb, wg_ref[grp])
        hu = _dot(xb, wu_ref[grp])
        gate_cols = jnp.concatenate(
            [jnp.broadcast_to(gs[:, j:j + 1] + gs[:, PER_GROUP + j:PER_GROUP + j + 1], (MOE_BM, D_EXPERT))
             for j in range(PER_GROUP)], axis=-1)
        a = (_silu(hg) * hu * gate_cols).astype(BF16)
        ys_ref[pl.ds(r0, MOE_BM), :] = _dot(a, wd_ref[grp]).astype(BF16)

    n_iter = lax.shift_right_logical(n_blk + (MOE_UNROLL - 1), MOE_UNROLL.bit_length() - 1)

    def blocks(i, carry):
        for u in range(MOE_UNROLL):
            block(i * MOE_UNROLL + u)
        return carry

    lax.fori_loop(0, n_iter, blocks, 0)

    def clear(b, carry):
        r0 = pl.multiple_of(b * MOE_BM, MOE_BM)
        ys_ref[pl.ds(r0, MOE_BM), :] = jnp.zeros((MOE_BM, D_MODEL), BF16)
        return carry

    def unsort_and_plan(width):
        lax.fori_loop(n_iter * MOE_UNROLL, width // MOE_BM, clear, 0)
        pos_col = jnp.broadcast_to(posf, (LANES, tm)).T
        lane = lax.broadcasted_iota(jnp.int32, (tm, LANES), 1).astype(F32)
        pt = jnp.concatenate(
            [jnp.where(pos_col == lane + float(LANES * k), 1.0, 0.0).astype(BF16) for k in range(width // LANES)],
            axis=-1)
        y = x_ref[...] + _dot(pt, ys_ref[0:width, :])
        if final:
            y = _rmsnorm(y, gf_ref[...])
        o_ref[...] = y
        park_plan(xn_ref[...])

    fits = n_iter * (MOE_UNROLL * MOE_BM) <= MOE_R_COMMON

    @pl.when(fits)
    def _():
        unsort_and_plan(MOE_R_COMMON)

    @pl.when(jnp.logical_not(fits))
    def _():
        unsort_and_plan(MOE_R)


def _moe(x2, layer, wts, experts, g_final, *, final):
    n_tiles = x2.shape[0] // MOE_TM
    assert n_tiles * MOE_TM == x2.shape[0] and MOE_R % (MOE_BM * MOE_UNROLL) == 0
    assert MOE_TM + N_GROUPS * (MOE_BM - 1) <= MOE_R
    x_spec = pl.BlockSpec((MOE_TM, D_MODEL), lambda i: (i, 0))
    next_spec = pl.BlockSpec((MOE_TM, D_MODEL), lambda i: (jnp.minimum(i + 1, n_tiles - 1), 0))
    names = ("g_ffn", "w_r", "b_r")
    return pl.pallas_call(
        functools.partial(_moe_kernel, final=final),
        grid=(n_tiles,),
        in_specs=([x_spec, next_spec] + [_layer_spec(wts[n], layer, pipeline_mode=pl.Buffered(1)) for n in names]
                  + [pl.BlockSpec(w.shape, lambda i: (0, 0, 0), pipeline_mode=pl.Buffered(1)) for w in experts]
                  + [pl.BlockSpec(g_final.shape, lambda i: (0, 0))]),
        out_specs=x_spec,
        out_shape=jax.ShapeDtypeStruct(x2.shape, F32),
        scratch_shapes=[pltpu.VMEM((MOE_R, D_MODEL), BF16),
                        pltpu.VMEM((MOE_TM, D_MODEL), BF16),
                        pltpu.VMEM((8, MOE_TM), F32),
                        pltpu.VMEM((GATE_ROWS, MOE_TM), BF16),
                        pltpu.SMEM((N_GROUPS,), jnp.int32)],
        compiler_params=pltpu.CompilerParams(dimension_semantics=("arbitrary",), vmem_limit_bytes=VMEM_LIMIT_BYTES),
        name="moe",
    )(x2, x2, *[wts[n] for n in names], *experts, g_final)


def kernel(x_prompt, x_sample, cache_mem_k, cache_mem_v, state_pool, mem_prompt, g_mix, w_in, g_v, w_s, b_s, w_pool, s_pool, w_out, g_mem, g_xattn, w_xq, w_xk, w_xv, w_xo, g_ffn, w_group, b_group, w_router, b_router, w_gate, w_up, w_down, g_final):
    bp, seq, _ = x_prompt.shape
    bs, dseq, _ = x_sample.shape

    mem_k, mem_v, mem_kb, mem_vb, w_in_b, w_out_b, w_xq_b, w_xo_b = _memkv(
        mem_prompt.reshape(bp * N_MEM, D_MODEL), g_mem.reshape(1, D_MODEL), w_xk, w_xv, w_in, w_out, w_xq, w_xo)
    mem_kb = mem_kb.reshape(DEPTH * bp, N_MEM, D_MODEL)
    mem_vb = mem_vb.reshape(DEPTH * bp, N_MEM, D_MODEL)

    pad_r = ROUTER_LANES - N_GROUPS - N_EXPERTS
    bias_full = jnp.repeat(jnp.swapaxes(b_s, 1, 2), HEAD_DIM_A, axis=2)
    wts = {
        "g_mix": g_mix.reshape(DEPTH, 1, D_MODEL), "w_in": w_in_b, "g_v": g_v.reshape(DEPTH, 1, WIDTH_A),
        "w_s": w_s.astype(BF16), "bias_p": bias_full,
        "coef_s": jnp.repeat(jnp.transpose(w_s[:, :, :dseq, :dseq], (0, 3, 2, 1)), HEAD_DIM_A, axis=3),
        "bias_s": bias_full[:, :dseq, :],
        "w_pool": w_pool.astype(BF16), "s_pool": s_pool.reshape(DEPTH, 1, WIDTH_B), "w_out": w_out_b,
        "g_xattn": g_xattn.reshape(DEPTH, 1, D_MODEL), "w_xq": w_xq_b, "w_xo": w_xo_b,
        "g_ffn": g_ffn.reshape(DEPTH, 1, D_MODEL),
        "w_r": jnp.concatenate([w_group, w_router, jnp.zeros((DEPTH, D_MODEL, pad_r), F32)], axis=-1).astype(BF16),
        "b_r": jnp.concatenate([b_group, b_router, jnp.zeros((DEPTH, pad_r), F32)], axis=-1).reshape(DEPTH, 1, ROUTER_LANES),
        "w_gate": w_gate, "w_up": w_up, "w_down": w_down,
    }
    g_final2 = g_final.reshape(1, D_MODEL)

    yp, ys = x_prompt, x_sample
    cv_p, cv_s, pl_p, pl_s = [], [], [], []
    for l in range(DEPTH):
        final = l == DEPTH - 1

        yp, cvp, plp = _mix(yp, None, l, wts, s_t=1, l_t=MIX_TM, pos0=0)
        ys, cvs, pls = _mix(ys, state_pool, l, wts, s_t=SAMPLE_MIX_SEQS, l_t=dseq, pos0=PAST_LEN)
        yp, ys, *experts = _xattn(yp, ys, mem_kb, mem_vb, cache_mem_k, cache_mem_v, l, wts, l_p=PROMPT_TM)
        yp = _moe(yp.reshape(bp * seq, D_MODEL), l, wts, experts, g_final2, final=final).reshape(bp, seq, D_MODEL)
        ys = _moe(ys.reshape(bs * dseq, D_MODEL), l, wts, experts, g_final2, final=final).reshape(bs, dseq, D_MODEL)

        cv_p.append(cvp.reshape(bp, CHUNK, HEADS_A, HEAD_DIM_A))
        cv_s.append(cvs.reshape(bs, dseq, HEADS_A, HEAD_DIM_A))
        pl_p.append(plp)
        pl_s.append(pls)

    return (yp, ys, mem_k, mem_v, jnp.stack(cv_p), jnp.stack(cv_s), jnp.stack(pl_p), jnp.stack(pl_s))
```

```python
import functools

import jax
import jax.numpy as jnp
from jax import lax
from jax.experimental import pallas as pl
from jax.experimental.pallas import tpu as pltpu

D_MODEL = 1024
DEPTH = 4
PAST_LEN = 16384
CHUNK = 128
WIDTH_A = 512
WIDTH_B = 512
HEADS_A = 8
HEAD_DIM_A = 64
POOL_WINDOWS = (2, 4, 8, 16)
POOL_GROUP_DIM = 128
POOL_HIST = 15
HALO = 16
PROJ_WIDTH = 2 * WIDTH_A + WIDTH_B
N_MEM = 256
X_HEADS = 4
X_HEAD_DIM = 256
N_GROUPS = 4
PER_GROUP = 4
N_EXPERTS = 16
D_EXPERT = 256
EPS = 1e-6
ROUTER_LANES = 128
LANES = 128
GATE_ROWS = 16

MEMKV_SEQS = 2
PROMPT_TM = 512
MIX_TM = 2048
MIX_SUB = 256
SAMPLE_MIX_SEQS = 32
SAMPLE_ATT_SEQS = 8
MOE_TM = 512
MOE_BM = 128
MOE_R = 1024
MOE_R_COMMON = 768
MOE_UNROLL = 2

F32 = jnp.float32
BF16 = jnp.bfloat16

VMEM_LIMIT_BYTES = 56 * 1024 * 1024


def _rmsnorm(x, g):
    r = lax.rsqrt(jnp.mean(x * x, axis=-1, keepdims=True) + EPS)
    return x * r * g


def _gelu_tanh(x):
    c = 0.7978845608028654
    return 0.5 * x * (1.0 + jnp.tanh(c * (x + 0.044715 * (x * x * x))))


def _silu(x):
    return x * (1.0 / (1.0 + jnp.exp(-x)))


def _dot(a, b):
    return jnp.dot(a, b, preferred_element_type=F32)


def _dot_nt(a, b):
    return lax.dot_general(a, b, (((1,), (1,)), ((), ())), preferred_element_type=F32)


def _layer_spec(arr, layer, **kw):
    nd = arr.ndim - 1
    return pl.BlockSpec((None,) + arr.shape[1:], lambda *_: (layer,) + (0,) * nd, **kw)


_PARAMS = pltpu.CompilerParams(dimension_semantics=("arbitrary", "arbitrary"), vmem_limit_bytes=VMEM_LIMIT_BYTES)


def _memkv_kernel(mem_ref, g_ref, wk_ref, wv_ref, win_ref, wout_ref, wq_ref, wo_ref, wpool_ref, spool_ref, woutb_ref,
                  k_ref, v_ref, kb_ref, vb_ref, win_o, wout_o, wq_o, wo_o, wfold_o, wkb_ref, wvb_ref):
    win_o[...] = win_ref[...].astype(BF16)
    wout_o[...] = wout_ref[...].astype(BF16)
    wq_o[...] = wq_ref[...].astype(BF16)
    wo_o[...] = wo_ref[...].astype(BF16)
    wfold_o[0] = jnp.dot(wpool_ref[0, 0] * spool_ref[0], woutb_ref[0], preferred_element_type=F32,
                         precision=lax.Precision.HIGHEST).astype(BF16)

    @pl.when(pl.program_id(1) == 0)
    def _():
        wkb_ref[...] = wk_ref[0].astype(BF16)
        wvb_ref[...] = wv_ref[0].astype(BF16)

    h = _rmsnorm(mem_ref[...], g_ref[...]).astype(BF16)
    k = _dot(h, wkb_ref[...])
    v = _dot(h, wvb_ref[...])
    for s in range(MEMKV_SEQS):
        rows = slice(s * N_MEM, (s + 1) * N_MEM)
        for hh in range(X_HEADS):
            sl = slice(hh * X_HEAD_DIM, (hh + 1) * X_HEAD_DIM)
            k_ref[0, s, :, hh, :] = k[rows, sl]
            v_ref[0, s, :, hh, :] = v[rows, sl]
    kb_ref[0] = k.astype(BF16)
    vb_ref[0] = v.astype(BF16)


def _memkv(mem2d, g_mem, wk, wv, w_in, w_out, w_xq, w_xo, w_pool, s_pool):
    rows = mem2d.shape[0]
    tile = MEMKV_SEQS * N_MEM
    n_r = rows // tile
    slab = D_MODEL // n_r
    n_pool = len(POOL_WINDOWS)
    assert n_r * tile == rows and slab * n_r == D_MODEL and n_r == n_pool
    first_b = WIDTH_A // POOL_GROUP_DIM
    fold_in = [pl.BlockSpec((1, 1, POOL_GROUP_DIM, POOL_GROUP_DIM), lambda l, r: (l, r, 0, 0)),
               pl.BlockSpec((1, 1, POOL_GROUP_DIM), lambda l, r: (l, 0, r)),
               pl.BlockSpec((1, POOL_GROUP_DIM, D_MODEL), lambda l, r: (l, first_b + r, 0))]
    fold_out = pl.BlockSpec((1, POOL_GROUP_DIM, D_MODEL), lambda l, r: (l, r, 0))
    out_sds = jax.ShapeDtypeStruct((DEPTH, rows // N_MEM, N_MEM, X_HEADS, X_HEAD_DIM), F32)
    outb_sds = jax.ShapeDtypeStruct((DEPTH, rows, D_MODEL), BF16)
    w_spec = pl.BlockSpec((1, D_MODEL, D_MODEL), lambda l, r: (l, 0, 0))
    o5_spec = pl.BlockSpec((1, MEMKV_SEQS, N_MEM, X_HEADS, X_HEAD_DIM), lambda l, r: (l, r, 0, 0, 0))
    o_spec = pl.BlockSpec((1, tile, D_MODEL), lambda l, r: (l, r, 0))
    dense = (w_in, w_out, w_xq, w_xo)
    slab_specs = [pl.BlockSpec((1, slab, w.shape[2]), lambda l, r: (l, r, 0)) for w in dense]
    return pl.pallas_call(
        _memkv_kernel,
        grid=(DEPTH, n_r),
        in_specs=[pl.BlockSpec((tile, D_MODEL), lambda l, r: (r, 0)),
                  pl.BlockSpec((1, D_MODEL), lambda l, r: (0, 0)),
                  w_spec, w_spec] + slab_specs + fold_in,
        out_specs=[o5_spec, o5_spec, o_spec, o_spec] + slab_specs + [fold_out],
        out_shape=([out_sds, out_sds, outb_sds, outb_sds] + [jax.ShapeDtypeStruct(w.shape, BF16) for w in dense]
                   + [jax.ShapeDtypeStruct((DEPTH, WIDTH_B, D_MODEL), BF16)]),
        scratch_shapes=[pltpu.VMEM((D_MODEL, D_MODEL), BF16), pltpu.VMEM((D_MODEL, D_MODEL), BF16)],
        compiler_params=_PARAMS,
        name="memkv",
    )(mem2d, g_mem, wk, wv, *dense, w_pool, s_pool.reshape(DEPTH, 1, WIDTH_B), w_out)


def _gate_chunks(v, ws_ref, tm):
    r_i = lax.broadcasted_iota(jnp.int32, (CHUNK, CHUNK), 0)
    c_i = lax.broadcasted_iota(jnp.int32, (CHUNK, CHUNK), 1)
    causal = c_i <= r_i
    lane = lax.broadcasted_iota(jnp.int32, (CHUNK, LANES), 1)
    zero_b = jnp.zeros((), BF16)
    vb = v.astype(BF16)
    pair_w = []
    for p in range(HEADS_A // 2):
        m_lo = jnp.where(causal, ws_ref[2 * p], zero_b)
        m_hi = jnp.where(causal, ws_ref[2 * p + 1], zero_b)
        pair_w.append(jnp.concatenate([m_lo, m_hi], axis=1))
    rows = []
    for c in range(tm // CHUNK):
        cols = []
        for p in range(HEADS_A // 2):
            vp = vb[c * CHUNK:(c + 1) * CHUNK, LANES * p:LANES * (p + 1)]
            lo = jnp.where(lane < HEAD_DIM_A, vp, zero_b)
            hi = jnp.where(lane >= HEAD_DIM_A, vp, zero_b)
            cols.append(_dot(pair_w[p], jnp.concatenate([lo, hi], axis=0)))
        rows.append(jnp.concatenate(cols, axis=-1))
    return jnp.concatenate(rows, axis=0)


def _gate_short(v3, coef_ref, l_t):
    i_i = lax.broadcasted_iota(jnp.int32, (l_t, WIDTH_A), 0)
    acc = None
    for jj in range(l_t):
        cj = jnp.where(i_i >= jj, coef_ref[jj], 0.0)
        term = cj[None, :, :] * v3[:, jj:jj + 1, :]
        acc = term if acc is None else acc + term
    return acc


def _trailing_sums(zx, n_new):
    outs = []
    for g, w in enumerate(POOL_WINDOWS):
        a = zx[:, :, g * POOL_GROUP_DIM:(g + 1) * POOL_GROUP_DIM]
        span = 1
        while span < w:
            n = a.shape[1]
            a = a[:, span:, :] + a[:, :n - span, :]
            span *= 2
        first = HALO + 1 - w
        outs.append(a[:, first:first + n_new, :])
    return outs


def _mix_kernel(*refs, s_t, l_t, pos0, has_hist):
    if has_hist:
        (x_ref, hist_ref, gmix_ref, win_ref, gv_ref, ws_ref, mb_ref, wout_ref, wfold_ref,
         x1_ref, v_ref, zt_ref, zext_ref) = refs
    else:
        (x_ref, gmix_ref, win_ref, gv_ref, ws_ref, mb_ref, wout_ref, wfold_ref,
         x1_ref, v_ref, zt_ref, zext_ref, carry_ref) = refs
    j = pl.program_id(1)
    sub = min(l_t, MIX_SUB)
    ts = s_t * sub

    if has_hist:
        zext_ref[:, HALO - POOL_HIST:HALO, :] = hist_ref[...]
    else:
        @pl.when(j == 0)
        def _():
            zext_ref[:, 0:HALO, :] = jnp.zeros((s_t, HALO, WIDTH_B), F32)

        @pl.when(j > 0)
        def _():
            zext_ref[:, 0:HALO, :] = carry_ref[...]

    for r0 in range(0, l_t, sub):
        x = x_ref[:, r0:r0 + sub, :].reshape(ts, D_MODEL)
        h = _rmsnorm(x, gmix_ref[...]).astype(BF16)
        proj = _dot(h, win_ref[...])
        ua = _gelu_tanh(proj[:, :2 * WIDTH_A])
        u = ua[:, :WIDTH_A]
        v = _rmsnorm(ua[:, WIDTH_A:], gv_ref[...])
        z3 = proj[:, 2 * WIDTH_A:].reshape(s_t, sub, WIDTH_B)

        if has_hist:
            mixed = _gate_short(v.reshape(s_t, sub, WIDTH_A), ws_ref, sub) + mb_ref[...][None, :, :]
        else:
            mixed = _gate_chunks(v, ws_ref, ts).reshape(ts // CHUNK, CHUNK, WIDTH_A) + mb_ref[...][None, :, :]
        a_out = u * mixed.reshape(ts, WIDTH_A)

        zext_ref[:, HALO + r0:HALO + r0 + sub, :] = z3
        sums = _trailing_sums(zext_ref[:, r0:r0 + HALO + sub, :], sub)
        pos = pos0 + j * l_t + r0 + lax.broadcasted_iota(jnp.int32, (1, sub, 1), 1)
        pooled = []
        for g, w in enumerate(POOL_WINDOWS):
            sl = slice(g * POOL_GROUP_DIM, (g + 1) * POOL_GROUP_DIM)
            inv_cnt = 1.0 / jnp.minimum(w, pos + 1).astype(F32)
            pooled.append((sums[g] * inv_cnt - z3[:, :, sl]).reshape(ts, POOL_GROUP_DIM))

        cat = jnp.concatenate([a_out] + pooled, axis=-1).astype(BF16)
        x1 = x + _dot(cat[:, :WIDTH_A], wout_ref[0:WIDTH_A, :]) + _dot(cat[:, WIDTH_A:], wfold_ref[...])
        x1_ref[:, r0:r0 + sub, :] = x1.reshape(s_t, sub, D_MODEL)
        if has_hist:
            v_ref[...] = v.reshape(s_t, sub, WIDTH_A)
        elif r0 + sub == l_t:
            v_ref[...] = v[ts - CHUNK:, :].reshape(1, CHUNK, WIDTH_A)

    zt_ref[...] = zext_ref[:, l_t + 1:l_t + HALO, :]
    if not has_hist:
        carry_ref[...] = zext_ref[:, l_t:l_t + HALO, :]


def _mix(x3, hist, layer, wts, *, s_t, l_t, pos0):
    s_all, l_all, _ = x3.shape
    has_hist = hist is not None
    grid = (s_all // s_t, l_all // l_t)
    x_spec = pl.BlockSpec((s_t, l_t, D_MODEL), lambda b, j: (b, j, 0))
    in_specs = [x_spec]
    args = [x3]
    if has_hist:
        in_specs.append(pl.BlockSpec((None, s_t, POOL_HIST, WIDTH_B), lambda b, j: (layer, b, 0, 0)))
        args.append(hist)
    gate_w = ("coef_s", "bias_s") if has_hist else ("w_s", "bias_p")
    for name in ("g_mix", "w_in", "g_v") + gate_w + ("w_out", "w_fold"):
        in_specs.append(_layer_spec(wts[name], layer))
        args.append(wts[name])
    if has_hist:
        v_shape, v_blk = (s_all, l_all, WIDTH_A), (s_t, l_t, WIDTH_A)
    else:
        v_shape, v_blk = (s_all, CHUNK, WIDTH_A), (1, CHUNK, WIDTH_A)
    out_shape = [jax.ShapeDtypeStruct(x3.shape, F32),
                 jax.ShapeDtypeStruct(v_shape, F32),
                 jax.ShapeDtypeStruct((s_all, POOL_HIST, WIDTH_B), F32)]
    out_specs = [x_spec,
                 pl.BlockSpec(v_blk, lambda b, j: (b, 0, 0)),
                 pl.BlockSpec((s_t, POOL_HIST, WIDTH_B), lambda b, j: (b, 0, 0))]
    scratch = [pltpu.VMEM((s_t, HALO + l_t, WIDTH_B), F32)]
    if not has_hist:
        scratch.append(pltpu.VMEM((s_t, HALO, WIDTH_B), F32))
    return pl.pallas_call(
        functools.partial(_mix_kernel, s_t=s_t, l_t=l_t, pos0=pos0, has_hist=has_hist),
        grid=grid, in_specs=in_specs, out_specs=out_specs, out_shape=out_shape,
        scratch_shapes=scratch, compiler_params=_PARAMS,
        name="mix_sample" if has_hist else "mix_prompt",
    )(*args)


def _softmax_rows(sc):
    m = jnp.max(sc, axis=-1, keepdims=True)
    e = jnp.exp(sc - m)
    return e * (1.0 / jnp.sum(e, axis=-1, keepdims=True))


def _attend_split(qs, k_ref, v_ref, s):
    heads = []
    for hh in range(X_HEADS):
        sl = slice(hh * X_HEAD_DIM, (hh + 1) * X_HEAD_DIM)
        p = _softmax_rows(_dot_nt(qs[:, sl].astype(BF16), k_ref[s, :, sl]))
        heads.append(_dot(p.astype(BF16), v_ref[s, :, sl]))
    return jnp.concatenate(heads, axis=-1)


def _attend_native(qs, k_ref, v_ref, s, l_t):
    q_rows = jnp.concatenate([qs[:, hh * X_HEAD_DIM:(hh + 1) * X_HEAD_DIM] for hh in range(X_HEADS)], axis=0)
    k2 = k_ref[0, s].reshape(N_MEM * X_HEADS, X_HEAD_DIM).astype(BF16)
    v2 = v_ref[0, s].reshape(N_MEM * X_HEADS, X_HEAD_DIM).astype(BF16)
    sc = _dot_nt(q_rows.astype(BF16), k2)
    r_i = lax.broadcasted_iota(jnp.int32, sc.shape, 0)
    c_i = lax.broadcasted_iota(jnp.int32, sc.shape, 1)
    same_head = (r_i // l_t) == (c_i % X_HEADS)
    p = _softmax_rows(jnp.where(same_head, sc, -jnp.inf))
    o_rows = _dot(p.astype(BF16), v2)
    return jnp.concatenate([o_rows[hh * l_t:(hh + 1) * l_t] for hh in range(X_HEADS)], axis=-1)


def _xattn_kernel(xp_ref, xs_ref, kp_ref, vp_ref, ks_ref, vs_ref, g_ref, wq_ref, wo_ref, eg_ref, eu_ref, ed_ref,
                  op_ref, os_ref, eg_o, eu_o, ed_o, *, l_p, s_s, l_s):
    eg_o[...] = eg_ref[...].astype(BF16)
    eu_o[...] = eu_ref[...].astype(BF16)
    ed_o[...] = ed_ref[...].astype(BF16)
    n_s = s_s * l_s
    x = jnp.concatenate([xp_ref[...].reshape(l_p, D_MODEL), xs_ref[...].reshape(n_s, D_MODEL)], axis=0)
    h = _rmsnorm(x, g_ref[...]).astype(BF16)
    q = _dot(h, wq_ref[...]) * (X_HEAD_DIM ** -0.5)
    outs = [_attend_split(q[:l_p], kp_ref, vp_ref, 0)]
    for s in range(s_s):
        outs.append(_attend_native(q[l_p + s * l_s:l_p + (s + 1) * l_s], ks_ref, vs_ref, s, l_s))
    o = jnp.concatenate(outs, axis=0)
    x2 = x + _dot(o.astype(BF16), wo_ref[...])
    op_ref[...] = x2[:l_p].reshape(1, l_p, D_MODEL)
    os_ref[...] = x2[l_p:].reshape(s_s, l_s, D_MODEL)


def _xattn(xp, xs, mem_kb, mem_vb, cache_k, cache_v, layer, wts, *, l_p):
    bp, seq, _ = xp.shape
    s_all, l_s, _ = xs.shape
    tiles = seq // l_p
    steps = bp * tiles
    s_s = s_all // steps
    parts = steps // N_EXPERTS
    assert s_s * steps == s_all and seq % l_p == 0 and parts * N_EXPERTS == steps
    rows_in, rows_mid = D_MODEL // parts, D_EXPERT // parts

    def step(b, j):
        return b * tiles + j

    xp_spec = pl.BlockSpec((1, l_p, D_MODEL), lambda b, j: (b, j, 0))
    xs_spec = pl.BlockSpec((s_s, l_s, D_MODEL), lambda b, j: (step(b, j), 0, 0))
    mp_spec = pl.BlockSpec((1, N_MEM, D_MODEL), lambda b, j: (layer * bp + b, 0, 0))
    ms_spec = pl.BlockSpec((1, s_s, N_MEM, X_HEADS, X_HEAD_DIM), lambda b, j: (layer, step(b, j), 0, 0, 0))
    gu_in = pl.BlockSpec((None, None, rows_in, D_EXPERT),
                         lambda b, j: (layer, step(b, j) // parts, step(b, j) % parts, 0))
    dn_in = pl.BlockSpec((None, None, rows_mid, D_MODEL),
                         lambda b, j: (layer, step(b, j) // parts, step(b, j) % parts, 0))
    gu_out = pl.BlockSpec((None, rows_in, D_EXPERT),
                          lambda b, j: (step(b, j) // parts // PER_GROUP, step(b, j) % parts,
                                        step(b, j) // parts % PER_GROUP))
    dn_out = pl.BlockSpec((None, rows_mid, D_MODEL),
                          lambda b, j: (step(b, j) // parts // PER_GROUP,
                                        (step(b, j) // parts % PER_GROUP) * parts + step(b, j) % parts, 0))
    names = ("g_xattn", "w_xq", "w_xo")
    gu_sds = jax.ShapeDtypeStruct((N_GROUPS, D_MODEL, PER_GROUP * D_EXPERT), BF16)
    dn_sds = jax.ShapeDtypeStruct((N_GROUPS, PER_GROUP * D_EXPERT, D_MODEL), BF16)
    return pl.pallas_call(
        functools.partial(_xattn_kernel, l_p=l_p, s_s=s_s, l_s=l_s),
        grid=(bp, tiles),
        in_specs=([xp_spec, xs_spec, mp_spec, mp_spec, ms_spec, ms_spec] + [_layer_spec(wts[n], layer) for n in names]
                  + [gu_in, gu_in, dn_in]),
        out_specs=[xp_spec, xs_spec, gu_out, gu_out, dn_out],
        out_shape=[jax.ShapeDtypeStruct(xp.shape, F32), jax.ShapeDtypeStruct(xs.shape, F32), gu_sds, gu_sds, dn_sds],
        compiler_params=_PARAMS,
        name="xattn",
    )(xp, xs, mem_kb, mem_vb, cache_k, cache_v, *[wts[n] for n in names],
      wts["w_gate"], wts["w_up"], wts["w_down"])


def _first_index(vals, target):
    idx = jnp.full(target.shape, len(vals) - 1, jnp.int32)
    for i in range(len(vals) - 2, -1, -1):
        idx = jnp.where(vals[i] == target, i, idx)
    return idx


def _max_of(vals):
    m = vals[0]
    for v in vals[1:]:
        m = jnp.maximum(m, v)
    return m


def _route(lt):
    rows = [lt[i:i + 1, :] for i in range(N_GROUPS + N_EXPERTS)]
    g_l = rows[:N_GROUPS]
    gmax = _max_of(g_l)
    gsum = g_l[0] * 0.0
    for v in g_l:
        gsum = gsum + jnp.exp(v - gmax)
    g_p = 1.0 / gsum
    g_idx = _first_index(g_l, gmax)
    e_sel = []
    for j in range(PER_GROUP):
        v = rows[N_GROUPS + PER_GROUP * (N_GROUPS - 1) + j]
        for g in range(N_GROUPS - 2, -1, -1):
            v = jnp.where(g_idx == g, rows[N_GROUPS + PER_GROUP * g + j], v)
        e_sel.append(v)
    v1 = _max_of(e_sel)
    i1 = _first_index(e_sel, v1)
    neg = jnp.float32(-jnp.inf)
    e_rest = [jnp.where(i1 == j, neg, e_sel[j]) for j in range(PER_GROUP)]
    v2 = _max_of(e_rest)
    i2 = _first_index(e_rest, v2)
    t = jnp.exp(v2 - v1)
    w1 = g_p / (1.0 + t)
    w2 = g_p * t / (1.0 + t)
    gates = [jnp.where(i1 == j, w1, 0.0) + jnp.where(i2 == j, w2, 0.0) for j in range(PER_GROUP)]
    return g_idx, gates


def _stack_rows(rows, n_rows, tm):
    ri = lax.broadcasted_iota(jnp.int32, (n_rows, tm), 0)
    out = jnp.zeros((n_rows, tm), F32)
    for i, r in enumerate(rows):
        out = jnp.where(ri == i, r, out)
    return out


def _moe_plan(x, g_ref, wr_ref, br_ref):
    tm = MOE_TM
    h = _rmsnorm(x, g_ref[...]).astype(BF16)
    logits = _dot(h, wr_ref[...]) + br_ref[...]
    g_idx, gates = _route(logits.T)

    member = [(g_idx == g).astype(F32) for g in range(N_GROUPS)]
    t_r = lax.broadcasted_iota(jnp.int32, (tm, tm), 0)
    t_c = lax.broadcasted_iota(jnp.int32, (tm, tm), 1)
    upper = jnp.where(t_r <= t_c, 1.0, 0.0).astype(BF16)
    incl = _dot(_stack_rows(member, GATE_ROWS, tm).astype(BF16), upper)
    end_blk = []
    posf = jnp.zeros((1, tm), F32)
    run = jnp.zeros((1, 1), F32)
    for g in range(N_GROUPS):
        n_g = jnp.sum(member[g], axis=-1, keepdims=True)
        posf = posf + member[g] * (run * MOE_BM + incl[g:g + 1, :] - 1.0)
        run = run + jnp.floor((n_g + (MOE_BM - 1)) * (1.0 / MOE_BM))
        end_blk.append(run)

    g_hi = [gt.astype(BF16).astype(F32) for gt in gates]
    g_lo = [gt - hi for gt, hi in zip(gates, g_hi)]
    gmat = _stack_rows(g_hi + g_lo, GATE_ROWS, tm).astype(BF16)
    return h, posf, gmat, end_blk


def _moe_kernel(x_ref, xn_ref, g_ref, wr_ref, br_ref, wg_ref, wu_ref, wd_ref, gf_ref, o_ref,
                ys_ref, h_ref, pos_ref, gmat_ref, cnt_ref, *, final):
    tm = MOE_TM

    def park_plan(x):
        h, posf, gmat, end_blk = _moe_plan(x, g_ref, wr_ref, br_ref)
        h_ref[...] = h
        pos_ref[...] = jnp.broadcast_to(posf, pos_ref.shape)
        gmat_ref[...] = gmat
        for k, e in enumerate(end_blk):
            cnt_ref[k] = e.astype(jnp.int32)[0, 0]

    @pl.when(pl.program_id(0) == 0)
    def _():
        park_plan(x_ref[...])

    posf = pos_ref[0:1, :]
    pos = posf.astype(jnp.int32)
    gmat = gmat_ref[...]
    e0, e1, e2, n_blk = [cnt_ref[k] for k in range(N_GROUPS)]

    r_iota = lax.broadcasted_iota(jnp.int32, (MOE_BM, tm), 0)

    def block(b):
        r0 = pl.multiple_of(b * MOE_BM, MOE_BM)
        grp = (b >= e0).astype(jnp.int32) + (b >= e1).astype(jnp.int32) + (b >= e2).astype(jnp.int32)
        pb = jnp.where(r_iota + r0 == pos, 1.0, 0.0).astype(BF16)
        xb = _dot(pb, h_ref[...]).astype(BF16)
        gs = _dot_nt(pb, gmat)
        hg = _dot(xb, wg_ref[grp])
        hu = _dot(xb, wu_ref[grp])
        gate_cols = jnp.concatenate(
            [jnp.broadcast_to(gs[:, j:j + 1] + gs[:, PER_GROUP + j:PER_GROUP + j + 1], (MOE_BM, D_EXPERT))
             for j in range(PER_GROUP)], axis=-1)
        a = (_silu(hg) * hu * gate_cols).astype(BF16)
        ys_ref[pl.ds(r0, MOE_BM), :] = _dot(a, wd_ref[grp]).astype(BF16)

    n_iter = lax.shift_right_logical(n_blk + (MOE_UNROLL - 1), MOE_UNROLL.bit_length() - 1)

    def blocks(i, carry):
        for u in range(MOE_UNROLL):
            block(i * MOE_UNROLL + u)
        return carry

    lax.fori_loop(0, n_iter, blocks, 0)

    def clear(b, carry):
        r0 = pl.multiple_of(b * MOE_BM, MOE_BM)
        ys_ref[pl.ds(r0, MOE_BM), :] = jnp.zeros((MOE_BM, D_MODEL), BF16)
        return carry

    def unsort_and_plan(width):
        lax.fori_loop(n_iter * MOE_UNROLL, width // MOE_BM, clear, 0)
        pos_col = jnp.broadcast_to(posf, (LANES, tm)).T
        lane = lax.broadcasted_iota(jnp.int32, (tm, LANES), 1).astype(F32)
        pt = jnp.concatenate(
            [jnp.where(pos_col == lane + float(LANES * k), 1.0, 0.0).astype(BF16) for k in range(width // LANES)],
            axis=-1)
        y = x_ref[...] + _dot(pt, ys_ref[0:width, :])
        if final:
            y = _rmsnorm(y, gf_ref[...])
        o_ref[...] = y
        park_plan(xn_ref[...])

    fits = n_iter * (MOE_UNROLL * MOE_BM) <= MOE_R_COMMON

    @pl.when(fits)
    def _():
        unsort_and_plan(MOE_R_COMMON)

    @pl.when(jnp.logical_not(fits))
    def _():
        unsort_and_plan(MOE_R)


def _moe(x2, layer, wts, experts, g_final, *, final):
    n_tiles = x2.shape[0] // MOE_TM
    assert n_tiles * MOE_TM == x2.shape[0] and MOE_R % (MOE_BM * MOE_UNROLL) == 0
    assert MOE_TM + N_GROUPS * (MOE_BM - 1) <= MOE_R
    x_spec = pl.BlockSpec((MOE_TM, D_MODEL), lambda i: (i, 0))
    next_spec = pl.BlockSpec((MOE_TM, D_MODEL), lambda i: (jnp.minimum(i + 1, n_tiles - 1), 0))
    names = ("g_ffn", "w_r", "b_r")
    return pl.pallas_call(
        functools.partial(_moe_kernel, final=final),
        grid=(n_tiles,),
        in_specs=([x_spec, next_spec] + [_layer_spec(wts[n], layer, pipeline_mode=pl.Buffered(1)) for n in names]
                  + [pl.BlockSpec(w.shape, lambda i: (0, 0, 0), pipeline_mode=pl.Buffered(1)) for w in experts]
                  + [pl.BlockSpec(g_final.shape, lambda i: (0, 0))]),
        out_specs=x_spec,
        out_shape=jax.ShapeDtypeStruct(x2.shape, F32),
        scratch_shapes=[pltpu.VMEM((MOE_R, D_MODEL), BF16),
                        pltpu.VMEM((MOE_TM, D_MODEL), BF16),
                        pltpu.VMEM((8, MOE_TM), F32),
                        pltpu.VMEM((GATE_ROWS, MOE_TM), BF16),
                        pltpu.SMEM((N_GROUPS,), jnp.int32)],
        compiler_params=pltpu.CompilerParams(dimension_semantics=("arbitrary",), vmem_limit_bytes=VMEM_LIMIT_BYTES),
        name="moe",
    )(x2, x2, *[wts[n] for n in names], *experts, g_final)


def kernel(x_prompt, x_sample, cache_mem_k, cache_mem_v, state_pool, mem_prompt, g_mix, w_in, g_v, w_s, b_s, w_pool, s_pool, w_out, g_mem, g_xattn, w_xq, w_xk, w_xv, w_xo, g_ffn, w_group, b_group, w_router, b_router, w_gate, w_up, w_down, g_final):
    bp, seq, _ = x_prompt.shape
    bs, dseq, _ = x_sample.shape

    mem_k, mem_v, mem_kb, mem_vb, w_in_b, w_out_b, w_xq_b, w_xo_b, w_fold = _memkv(
        mem_prompt.reshape(bp * N_MEM, D_MODEL), g_mem.reshape(1, D_MODEL), w_xk, w_xv, w_in, w_out, w_xq, w_xo,
        w_pool, s_pool)
    mem_kb = mem_kb.reshape(DEPTH * bp, N_MEM, D_MODEL)
    mem_vb = mem_vb.reshape(DEPTH * bp, N_MEM, D_MODEL)

    pad_r = ROUTER_LANES - N_GROUPS - N_EXPERTS
    bias_full = jnp.repeat(jnp.swapaxes(b_s, 1, 2), HEAD_DIM_A, axis=2)
    wts = {
        "g_mix": g_mix.reshape(DEPTH, 1, D_MODEL), "w_in": w_in_b, "g_v": g_v.reshape(DEPTH, 1, WIDTH_A),
        "w_s": w_s.astype(BF16), "bias_p": bias_full,
        "coef_s": jnp.repeat(jnp.transpose(w_s[:, :, :dseq, :dseq], (0, 3, 2, 1)), HEAD_DIM_A, axis=3),
        "bias_s": bias_full[:, :dseq, :],
        "w_out": w_out_b, "w_fold": w_fold,
        "g_xattn": g_xattn.reshape(DEPTH, 1, D_MODEL), "w_xq": w_xq_b, "w_xo": w_xo_b,
        "g_ffn": g_ffn.reshape(DEPTH, 1, D_MODEL),
        "w_r": jnp.concatenate([w_group, w_router, jnp.zeros((DEPTH, D_MODEL, pad_r), F32)], axis=-1).astype(BF16),
        "b_r": jnp.concatenate([b_group, b_router, jnp.zeros((DEPTH, pad_r), F32)], axis=-1).reshape(DEPTH, 1, ROUTER_LANES),
        "w_gate": w_gate, "w_up": w_up, "w_down": w_down,
    }
    g_final2 = g_final.reshape(1, D_MODEL)

    yp, ys = x_prompt, x_sample
    cv_p, cv_s, pl_p, pl_s = [], [], [], []
    for l in range(DEPTH):
        final = l == DEPTH - 1

        yp, cvp, plp = _mix(yp, None, l, wts, s_t=1, l_t=MIX_TM, pos0=0)
        ys, cvs, pls = _mix(ys, state_pool, l, wts, s_t=SAMPLE_MIX_SEQS, l_t=dseq, pos0=PAST_LEN)
        yp, ys, *experts = _xattn(yp, ys, mem_kb, mem_vb, cache_mem_k, cache_mem_v, l, wts, l_p=PROMPT_TM)
        yp = _moe(yp.reshape(bp * seq, D_MODEL), l, wts, experts, g_final2, final=final).reshape(bp, seq, D_MODEL)
        ys = _moe(ys.reshape(bs * dseq, D_MODEL), l, wts, experts, g_final2, final=final).reshape(bs, dseq, D_MODEL)

        cv_p.append(cvp.reshape(bp, CHUNK, HEADS_A, HEAD_DIM_A))
        cv_s.append(cvs.reshape(bs, dseq, HEADS_A, HEAD_DIM_A))
        pl_p.append(plp)
        pl_s.append(pls)

    return (yp, ys, mem_k, mem_v, jnp.stack(cv_p), jnp.stack(cv_s), jnp.stack(pl_p), jnp.stack(pl_s))
```

```python
import functools

import jax
import jax.numpy as jnp
from jax import lax
from jax.experimental import pallas as pl
from jax.experimental.pallas import tpu as pltpu

D_MODEL = 1024
DEPTH = 4
PAST_LEN = 16384
CHUNK = 128
WIDTH_A = 512
WIDTH_B = 512
HEADS_A = 8
HEAD_DIM_A = 64
POOL_WINDOWS = (2, 4, 8, 16)
POOL_GROUP_DIM = 128
POOL_HIST = 15
HALO = 16
N_MEM = 256
X_HEADS = 4
X_HEAD_DIM = 256
N_GROUPS = 4
PER_GROUP = 4
N_EXPERTS = 16
D_EXPERT = 256
EPS = 1e-6
ROUTER_LANES = 128
LANES = 128
GATE_ROWS = 16

MEMKV_SEQS = 2
PROMPT_TM = 512
MIX_TM = 2048
MIX_SUB = 256
SAMPLE_MIX_SEQS = 32
MOE_TM = 512
MOE_BM = 128
MOE_R = 1024
MOE_R_COMMON = 768
MOE_UNROLL = 2

F32 = jnp.float32
BF16 = jnp.bfloat16

VMEM_LIMIT_BYTES = 56 * 1024 * 1024


def _rmsnorm(x, g):
    r = lax.rsqrt(jnp.mean(x * x, axis=-1, keepdims=True) + EPS)
    return x * r * g


def _gelu_tanh(x):
    c = 0.7978845608028654
    return 0.5 * x * (1.0 + jnp.tanh(c * (x + 0.044715 * (x * x * x))))


def _silu(x):
    return x * (1.0 / (1.0 + jnp.exp(-x)))


def _dot(a, b):
    return jnp.dot(a, b, preferred_element_type=F32)


def _dot_nt(a, b):
    return lax.dot_general(a, b, (((1,), (1,)), ((), ())), preferred_element_type=F32)


def _layer_spec(arr, layer, **kw):
    nd = arr.ndim - 1
    return pl.BlockSpec((None,) + arr.shape[1:], lambda *_: (layer,) + (0,) * nd, **kw)


_PARAMS = pltpu.CompilerParams(dimension_semantics=("arbitrary", "arbitrary"), vmem_limit_bytes=VMEM_LIMIT_BYTES)


def _memkv_kernel(mem_ref, g_ref, wk_ref, wv_ref, win_ref, wout_ref, wq_ref, wo_ref, wpool_ref, spool_ref, woutb_ref,
                  k_ref, v_ref, kb_ref, vb_ref, win_o, wout_o, wq_o, wo_o, wfold_o, wkb_ref, wvb_ref):
    win_o[...] = win_ref[...].astype(BF16)
    wout_o[...] = wout_ref[...].astype(BF16)
    wq_o[...] = wq_ref[...].astype(BF16)
    wo_o[...] = wo_ref[...].astype(BF16)
    wfold_o[0] = jnp.dot(wpool_ref[0, 0] * spool_ref[0], woutb_ref[0], preferred_element_type=F32,
                         precision=lax.Precision.HIGHEST).astype(BF16)

    @pl.when(pl.program_id(1) == 0)
    def _():
        wkb_ref[...] = wk_ref[0].astype(BF16)
        wvb_ref[...] = wv_ref[0].astype(BF16)

    h = _rmsnorm(mem_ref[...], g_ref[...]).astype(BF16)
    k = _dot(h, wkb_ref[...])
    v = _dot(h, wvb_ref[...])
    for s in range(MEMKV_SEQS):
        rows = slice(s * N_MEM, (s + 1) * N_MEM)
        for hh in range(X_HEADS):
            sl = slice(hh * X_HEAD_DIM, (hh + 1) * X_HEAD_DIM)
            k_ref[0, s, :, hh, :] = k[rows, sl]
            v_ref[0, s, :, hh, :] = v[rows, sl]
    kb_ref[0] = k.astype(BF16)
    vb_ref[0] = v.astype(BF16)


def _memkv(mem2d, g_mem, wk, wv, w_in, w_out, w_xq, w_xo, w_pool, s_pool):
    rows = mem2d.shape[0]
    tile = MEMKV_SEQS * N_MEM
    n_r = rows // tile
    slab = D_MODEL // n_r
    n_pool = len(POOL_WINDOWS)
    assert n_r * tile == rows and slab * n_r == D_MODEL and n_r == n_pool
    first_b = WIDTH_A // POOL_GROUP_DIM
    fold_in = [pl.BlockSpec((1, 1, POOL_GROUP_DIM, POOL_GROUP_DIM), lambda l, r: (l, r, 0, 0)),
               pl.BlockSpec((1, 1, POOL_GROUP_DIM), lambda l, r: (l, 0, r)),
               pl.BlockSpec((1, POOL_GROUP_DIM, D_MODEL), lambda l, r: (l, first_b + r, 0))]
    fold_out = pl.BlockSpec((1, POOL_GROUP_DIM, D_MODEL), lambda l, r: (l, r, 0))
    out_sds = jax.ShapeDtypeStruct((DEPTH, rows // N_MEM, N_MEM, X_HEADS, X_HEAD_DIM), F32)
    outb_sds = jax.ShapeDtypeStruct((DEPTH, rows, D_MODEL), BF16)
    w_spec = pl.BlockSpec((1, D_MODEL, D_MODEL), lambda l, r: (l, 0, 0))
    o5_spec = pl.BlockSpec((1, MEMKV_SEQS, N_MEM, X_HEADS, X_HEAD_DIM), lambda l, r: (l, r, 0, 0, 0))
    o_spec = pl.BlockSpec((1, tile, D_MODEL), lambda l, r: (l, r, 0))
    dense = (w_in, w_out, w_xq, w_xo)
    slab_specs = [pl.BlockSpec((1, slab, w.shape[2]), lambda l, r: (l, r, 0)) for w in dense]
    return pl.pallas_call(
        _memkv_kernel,
        grid=(DEPTH, n_r),
        in_specs=[pl.BlockSpec((tile, D_MODEL), lambda l, r: (r, 0)),
                  pl.BlockSpec((1, D_MODEL), lambda l, r: (0, 0)),
                  w_spec, w_spec] + slab_specs + fold_in,
        out_specs=[o5_spec, o5_spec, o_spec, o_spec] + slab_specs + [fold_out],
        out_shape=([out_sds, out_sds, outb_sds, outb_sds] + [jax.ShapeDtypeStruct(w.shape, BF16) for w in dense]
                   + [jax.ShapeDtypeStruct((DEPTH, WIDTH_B, D_MODEL), BF16)]),
        scratch_shapes=[pltpu.VMEM((D_MODEL, D_MODEL), BF16), pltpu.VMEM((D_MODEL, D_MODEL), BF16)],
        compiler_params=_PARAMS,
        name="memkv",
    )(mem2d, g_mem, wk, wv, *dense, w_pool, s_pool.reshape(DEPTH, 1, WIDTH_B), w_out)


def _gate_chunks(v, ws_ref, tm):
    r_i = lax.broadcasted_iota(jnp.int32, (CHUNK, CHUNK), 0)
    c_i = lax.broadcasted_iota(jnp.int32, (CHUNK, CHUNK), 1)
    causal = c_i <= r_i
    lane = lax.broadcasted_iota(jnp.int32, (CHUNK, LANES), 1)
    zero_b = jnp.zeros((), BF16)
    vb = v.astype(BF16)
    pair_w = []
    for p in range(HEADS_A // 2):
        m_lo = jnp.where(causal, ws_ref[2 * p], zero_b)
        m_hi = jnp.where(causal, ws_ref[2 * p + 1], zero_b)
        pair_w.append(jnp.concatenate([m_lo, m_hi], axis=1))
    rows = []
    for c in range(tm // CHUNK):
        cols = []
        for p in range(HEADS_A // 2):
            vp = vb[c * CHUNK:(c + 1) * CHUNK, LANES * p:LANES * (p + 1)]
            lo = jnp.where(lane < HEAD_DIM_A, vp, zero_b)
            hi = jnp.where(lane >= HEAD_DIM_A, vp, zero_b)
            cols.append(_dot(pair_w[p], jnp.concatenate([lo, hi], axis=0)))
        rows.append(jnp.concatenate(cols, axis=-1))
    return jnp.concatenate(rows, axis=0)


def _gate_short(v3, coef_ref, l_t):
    i_i = lax.broadcasted_iota(jnp.int32, (l_t, WIDTH_A), 0)
    acc = None
    for jj in range(l_t):
        cj = jnp.where(i_i >= jj, coef_ref[jj], 0.0)
        term = cj[None, :, :] * v3[:, jj:jj + 1, :]
        acc = term if acc is None else acc + term
    return acc


def _trailing_sums(zx, n_new):
    outs = []
    for g, w in enumerate(POOL_WINDOWS):
        a = zx[:, :, g * POOL_GROUP_DIM:(g + 1) * POOL_GROUP_DIM]
        span = 1
        while span < w:
            n = a.shape[1]
            a = a[:, span:, :] + a[:, :n - span, :]
            span *= 2
        first = HALO + 1 - w
        outs.append(a[:, first:first + n_new, :])
    return outs


def _mix_kernel(*refs, s_t, l_t, pos0, has_hist):
    if has_hist:
        (x_ref, hist_ref, gmix_ref, win_ref, gv_ref, ws_ref, mb_ref, wout_ref, wfold_ref,
         x1_ref, v_ref, zt_ref, zext_ref) = refs
    else:
        (x_ref, gmix_ref, win_ref, gv_ref, ws_ref, mb_ref, wout_ref, wfold_ref,
         x1_ref, v_ref, zt_ref, zext_ref, carry_ref) = refs
    j = pl.program_id(1)
    sub = min(l_t, MIX_SUB)
    ts = s_t * sub

    if has_hist:
        zext_ref[:, HALO - POOL_HIST:HALO, :] = hist_ref[...]
    else:
        @pl.when(j == 0)
        def _():
            zext_ref[:, 0:HALO, :] = jnp.zeros((s_t, HALO, WIDTH_B), F32)

        @pl.when(j > 0)
        def _():
            zext_ref[:, 0:HALO, :] = carry_ref[...]

    for r0 in range(0, l_t, sub):
        x = x_ref[:, r0:r0 + sub, :].reshape(ts, D_MODEL)
        h = _rmsnorm(x, gmix_ref[...]).astype(BF16)
        proj = _dot(h, win_ref[...])
        ua = _gelu_tanh(proj[:, :2 * WIDTH_A])
        u = ua[:, :WIDTH_A]
        v = _rmsnorm(ua[:, WIDTH_A:], gv_ref[...])
        z3 = proj[:, 2 * WIDTH_A:].reshape(s_t, sub, WIDTH_B)

        if has_hist:
            mixed = _gate_short(v.reshape(s_t, sub, WIDTH_A), ws_ref, sub) + mb_ref[...][None, :, :]
        else:
            mixed = _gate_chunks(v, ws_ref, ts).reshape(ts // CHUNK, CHUNK, WIDTH_A) + mb_ref[...][None, :, :]
        a_out = u * mixed.reshape(ts, WIDTH_A)

        zext_ref[:, HALO + r0:HALO + r0 + sub, :] = z3
        sums = _trailing_sums(zext_ref[:, r0:r0 + HALO + sub, :], sub)
        pos = pos0 + j * l_t + r0 + lax.broadcasted_iota(jnp.int32, (1, sub, 1), 1)
        pooled = []
        for g, w in enumerate(POOL_WINDOWS):
            sl = slice(g * POOL_GROUP_DIM, (g + 1) * POOL_GROUP_DIM)
            inv_cnt = 1.0 / jnp.minimum(w, pos + 1).astype(F32)
            pooled.append((sums[g] * inv_cnt - z3[:, :, sl]).reshape(ts, POOL_GROUP_DIM))

        cat = jnp.concatenate([a_out] + pooled, axis=-1).astype(BF16)
        x1 = x + _dot(cat[:, :WIDTH_A], wout_ref[0:WIDTH_A, :]) + _dot(cat[:, WIDTH_A:], wfold_ref[...])
        x1_ref[:, r0:r0 + sub, :] = x1.reshape(s_t, sub, D_MODEL)
        if has_hist:
            v_ref[...] = v.reshape(s_t, sub, WIDTH_A)
        elif r0 + sub == l_t:
            v_ref[...] = v[ts - CHUNK:, :].reshape(1, CHUNK, WIDTH_A)

    zt_ref[...] = zext_ref[:, l_t + 1:l_t + HALO, :]
    if not has_hist:
        carry_ref[...] = zext_ref[:, l_t:l_t + HALO, :]


def _mix(x3, hist, layer, wts, *, s_t, l_t, pos0):
    s_all, l_all, _ = x3.shape
    has_hist = hist is not None
    grid = (s_all // s_t, l_all // l_t)
    x_spec = pl.BlockSpec((s_t, l_t, D_MODEL), lambda b, j: (b, j, 0))
    in_specs = [x_spec]
    args = [x3]
    if has_hist:
        in_specs.append(pl.BlockSpec((None, s_t, POOL_HIST, WIDTH_B), lambda b, j: (layer, b, 0, 0)))
        args.append(hist)
    gate_w = ("coef_s", "bias_s") if has_hist else ("w_s", "bias_p")
    for name in ("g_mix", "w_in", "g_v") + gate_w + ("w_out", "w_fold"):
        in_specs.append(_layer_spec(wts[name], layer))
        args.append(wts[name])
    if has_hist:
        v_shape, v_blk = (s_all, l_all, WIDTH_A), (s_t, l_t, WIDTH_A)
    else:
        v_shape, v_blk = (s_all, CHUNK, WIDTH_A), (1, CHUNK, WIDTH_A)
    out_shape = [jax.ShapeDtypeStruct(x3.shape, F32),
                 jax.ShapeDtypeStruct(v_shape, F32),
                 jax.ShapeDtypeStruct((s_all, POOL_HIST, WIDTH_B), F32)]
    out_specs = [x_spec,
                 pl.BlockSpec(v_blk, lambda b, j: (b, 0, 0)),
                 pl.BlockSpec((s_t, POOL_HIST, WIDTH_B), lambda b, j: (b, 0, 0))]
    scratch = [pltpu.VMEM((s_t, HALO + l_t, WIDTH_B), F32)]
    if not has_hist:
        scratch.append(pltpu.VMEM((s_t, HALO, WIDTH_B), F32))
    return pl.pallas_call(
        functools.partial(_mix_kernel, s_t=s_t, l_t=l_t, pos0=pos0, has_hist=has_hist),
        grid=grid, in_specs=in_specs, out_specs=out_specs, out_shape=out_shape,
        scratch_shapes=scratch, compiler_params=_PARAMS,
        name="mix_sample" if has_hist else "mix_prompt",
    )(*args)


def _softmax_rows(sc):
    m = jnp.max(sc, axis=-1, keepdims=True)
    e = jnp.exp(sc - m)
    return e * (1.0 / jnp.sum(e, axis=-1, keepdims=True))


def _native_rows(m_ref, s):
    return m_ref[0, s].reshape(N_MEM * X_HEADS, X_HEAD_DIM).astype(BF16)


def _attend_staged(q, l_p, s_s, l_s, kp_ref, vp_ref, ks_ref, vs_ref):
    sls = [slice(hh * X_HEAD_DIM, (hh + 1) * X_HEAD_DIM) for hh in range(X_HEADS)]
    qp = q[:l_p]
    q_rows = [jnp.concatenate([q[l_p + s * l_s:l_p + (s + 1) * l_s, sl] for sl in sls], axis=0) for s in range(s_s)]
    sc_p = [_dot_nt(qp[:, sl].astype(BF16), kp_ref[0, :, sl]) for sl in sls]
    sc_s = [_dot_nt(q_rows[s].astype(BF16), _native_rows(ks_ref, s)) for s in range(s_s)]
    r_i = lax.broadcasted_iota(jnp.int32, sc_s[0].shape, 0)
    c_i = lax.broadcasted_iota(jnp.int32, sc_s[0].shape, 1)
    same_head = (r_i // l_s) == (c_i % X_HEADS)
    pr_p = [_softmax_rows(sc).astype(BF16) for sc in sc_p]
    pr_s = [_softmax_rows(jnp.where(same_head, sc, -jnp.inf)).astype(BF16) for sc in sc_s]
    outs = [jnp.concatenate([_dot(p, vp_ref[0, :, sl]) for p, sl in zip(pr_p, sls)], axis=-1)]
    for s in range(s_s):
        o_rows = _dot(pr_s[s], _native_rows(vs_ref, s))
        outs.append(jnp.concatenate([o_rows[hh * l_s:(hh + 1) * l_s] for hh in range(X_HEADS)], axis=-1))
    return jnp.concatenate(outs, axis=0)


def _xattn_kernel(xp_ref, xs_ref, kp_ref, vp_ref, ks_ref, vs_ref, g_ref, wq_ref, wo_ref, eg_ref, eu_ref, ed_ref,
                  op_ref, os_ref, eg_o, eu_o, ed_o, *, l_p, s_s, l_s):
    eg_o[...] = eg_ref[...].astype(BF16)
    eu_o[...] = eu_ref[...].astype(BF16)
    ed_o[...] = ed_ref[...].astype(BF16)
    n_s = s_s * l_s
    x = jnp.concatenate([xp_ref[...].reshape(l_p, D_MODEL), xs_ref[...].reshape(n_s, D_MODEL)], axis=0)
    h = _rmsnorm(x, g_ref[...]).astype(BF16)
    q = _dot(h, wq_ref[...]) * (X_HEAD_DIM ** -0.5)
    o = _attend_staged(q, l_p, s_s, l_s, kp_ref, vp_ref, ks_ref, vs_ref)
    x2 = x + _dot(o.astype(BF16), wo_ref[...])
    op_ref[...] = x2[:l_p].reshape(1, l_p, D_MODEL)
    os_ref[...] = x2[l_p:].reshape(s_s, l_s, D_MODEL)


def _xattn(xp, xs, mem_kb, mem_vb, cache_k, cache_v, layer, wts, *, l_p):
    bp, seq, _ = xp.shape
    s_all, l_s, _ = xs.shape
    tiles = seq // l_p
    steps = bp * tiles
    s_s = s_all // steps
    parts = steps // N_EXPERTS
    assert s_s * steps == s_all and seq % l_p == 0 and parts * N_EXPERTS == steps
    rows_in, rows_mid = D_MODEL // parts, D_EXPERT // parts

    def step(b, j):
        return b * tiles + j

    xp_spec = pl.BlockSpec((1, l_p, D_MODEL), lambda b, j: (b, j, 0))
    xs_spec = pl.BlockSpec((s_s, l_s, D_MODEL), lambda b, j: (step(b, j), 0, 0))
    mp_spec = pl.BlockSpec((1, N_MEM, D_MODEL), lambda b, j: (layer * bp + b, 0, 0))
    ms_spec = pl.BlockSpec((1, s_s, N_MEM, X_HEADS, X_HEAD_DIM), lambda b, j: (layer, step(b, j), 0, 0, 0))
    gu_in = pl.BlockSpec((None, None, rows_in, D_EXPERT),
                         lambda b, j: (layer, step(b, j) // parts, step(b, j) % parts, 0))
    dn_in = pl.BlockSpec((None, None, rows_mid, D_MODEL),
                         lambda b, j: (layer, step(b, j) // parts, step(b, j) % parts, 0))
    gu_out = pl.BlockSpec((None, rows_in, D_EXPERT),
                          lambda b, j: (step(b, j) // parts // PER_GROUP, step(b, j) % parts,
                                        step(b, j) // parts % PER_GROUP))
    dn_out = pl.BlockSpec((None, rows_mid, D_MODEL),
                          lambda b, j: (step(b, j) // parts // PER_GROUP,
                                        (step(b, j) // parts % PER_GROUP) * parts + step(b, j) % parts, 0))
    names = ("g_xattn", "w_xq", "w_xo")
    gu_sds = jax.ShapeDtypeStruct((N_GROUPS, D_MODEL, PER_GROUP * D_EXPERT), BF16)
    dn_sds = jax.ShapeDtypeStruct((N_GROUPS, PER_GROUP * D_EXPERT, D_MODEL), BF16)
    return pl.pallas_call(
        functools.partial(_xattn_kernel, l_p=l_p, s_s=s_s, l_s=l_s),
        grid=(bp, tiles),
        in_specs=([xp_spec, xs_spec, mp_spec, mp_spec, ms_spec, ms_spec] + [_layer_spec(wts[n], layer) for n in names]
                  + [gu_in, gu_in, dn_in]),
        out_specs=[xp_spec, xs_spec, gu_out, gu_out, dn_out],
        out_shape=[jax.ShapeDtypeStruct(xp.shape, F32), jax.ShapeDtypeStruct(xs.shape, F32), gu_sds, gu_sds, dn_sds],
        compiler_params=_PARAMS,
        name="xattn",
    )(xp, xs, mem_kb, mem_vb, cache_k, cache_v, *[wts[n] for n in names],
      wts["w_gate"], wts["w_up"], wts["w_down"])


def _first_index(vals, target):
    idx = jnp.full(target.shape, len(vals) - 1, jnp.int32)
    for i in range(len(vals) - 2, -1, -1):
        idx = jnp.where(vals[i] == target, i, idx)
    return idx


def _max_of(vals):
    m = vals[0]
    for v in vals[1:]:
        m = jnp.maximum(m, v)
    return m


def _route(lt):
    rows = [lt[i:i + 1, :] for i in range(N_GROUPS + N_EXPERTS)]
    g_l = rows[:N_GROUPS]
    gmax = _max_of(g_l)
    gsum = g_l[0] * 0.0
    for v in g_l:
        gsum = gsum + jnp.exp(v - gmax)
    g_p = 1.0 / gsum
    g_idx = _first_index(g_l, gmax)
    e_sel = []
    for j in range(PER_GROUP):
        v = rows[N_GROUPS + PER_GROUP * (N_GROUPS - 1) + j]
        for g in range(N_GROUPS - 2, -1, -1):
            v = jnp.where(g_idx == g, rows[N_GROUPS + PER_GROUP * g + j], v)
        e_sel.append(v)
    v1 = _max_of(e_sel)
    i1 = _first_index(e_sel, v1)
    neg = jnp.float32(-jnp.inf)
    e_rest = [jnp.where(i1 == j, neg, e_sel[j]) for j in range(PER_GROUP)]
    v2 = _max_of(e_rest)
    i2 = _first_index(e_rest, v2)
    t = jnp.exp(v2 - v1)
    w1 = g_p / (1.0 + t)
    w2 = g_p * t / (1.0 + t)
    gates = [jnp.where(i1 == j, w1, 0.0) + jnp.where(i2 == j, w2, 0.0) for j in range(PER_GROUP)]
    return g_idx, gates


def _stack_rows(rows, n_rows, tm):
    ri = lax.broadcasted_iota(jnp.int32, (n_rows, tm), 0)
    out = jnp.zeros((n_rows, tm), F32)
    for i, r in enumerate(rows):
        out = jnp.where(ri == i, r, out)
    return out


def _moe_plan(x, g_ref, wr_ref, br_ref):
    tm = MOE_TM
    h = _rmsnorm(x, g_ref[...]).astype(BF16)
    logits = _dot(h, wr_ref[...]) + br_ref[...]
    g_idx, gates = _route(logits.T)

    member = [(g_idx == g).astype(F32) for g in range(N_GROUPS)]
    t_r = lax.broadcasted_iota(jnp.int32, (tm, tm), 0)
    t_c = lax.broadcasted_iota(jnp.int32, (tm, tm), 1)
    upper = jnp.where(t_r <= t_c, 1.0, 0.0).astype(BF16)
    incl = _dot(_stack_rows(member, GATE_ROWS, tm).astype(BF16), upper)
    end_blk = []
    posf = jnp.zeros((1, tm), F32)
    run = jnp.zeros((1, 1), F32)
    for g in range(N_GROUPS):
        n_g = jnp.sum(member[g], axis=-1, keepdims=True)
        posf = posf + member[g] * (run * MOE_BM + incl[g:g + 1, :] - 1.0)
        run = run + jnp.floor((n_g + (MOE_BM - 1)) * (1.0 / MOE_BM))
        end_blk.append(run)

    g_hi = [gt.astype(BF16).astype(F32) for gt in gates]
    g_lo = [gt - hi for gt, hi in zip(gates, g_hi)]
    gmat = _stack_rows(g_hi + g_lo, GATE_ROWS, tm).astype(BF16)
    return h, posf, gmat, end_blk


def _moe_kernel(x_ref, xn_ref, g_ref, wr_ref, br_ref, wg_ref, wu_ref, wd_ref, gf_ref, o_ref,
                ys_ref, h_ref, pos_ref, gmat_ref, cnt_ref, *, final):
    tm = MOE_TM

    def park_plan(x):
        h, posf, gmat, end_blk = _moe_plan(x, g_ref, wr_ref, br_ref)
        h_ref[...] = h
        pos_ref[...] = jnp.broadcast_to(posf, pos_ref.shape)
        gmat_ref[...] = gmat
        for k, e in enumerate(end_blk):
            cnt_ref[k] = e.astype(jnp.int32)[0, 0]

    @pl.when(pl.program_id(0) == 0)
    def _():
        park_plan(x_ref[...])

    posf = pos_ref[0:1, :]
    pos = posf.astype(jnp.int32)
    gmat = gmat_ref[...]
    e0, e1, e2, n_blk = [cnt_ref[k] for k in range(N_GROUPS)]

    r_iota = lax.broadcasted_iota(jnp.int32, (MOE_BM, tm), 0)

    def block(b):
        r0 = pl.multiple_of(b * MOE_BM, MOE_BM)
        grp = (b >= e0).astype(jnp.int32) + (b >= e1).astype(jnp.int32) + (b >= e2).astype(jnp.int32)
        pb = jnp.where(r_iota + r0 == pos, 1.0, 0.0).astype(BF16)
        xb = _dot(pb, h_ref[...]).astype(BF16)
        gs = _dot_nt(pb, gmat)
        hg = _dot(xb, wg_ref[grp])
        hu = _dot(xb, wu_ref[grp])
        gate_cols = jnp.concatenate(
            [jnp.broadcast_to(gs[:, j:j + 1] + gs[:, PER_GROUP + j:PER_GROUP + j + 1], (MOE_BM, D_EXPERT))
             for j in range(PER_GROUP)], axis=-1)
        a = (_silu(hg) * hu * gate_cols).astype(BF16)
        ys_ref[pl.ds(r0, MOE_BM), :] = _dot(a, wd_ref[grp]).astype(BF16)

    n_iter = lax.shift_right_logical(n_blk + (MOE_UNROLL - 1), MOE_UNROLL.bit_length() - 1)

    def blocks(i, carry):
        for u in range(MOE_UNROLL):
            block(i * MOE_UNROLL + u)
        return carry

    lax.fori_loop(0, n_iter, blocks, 0)

    def clear(b, carry):
        r0 = pl.multiple_of(b * MOE_BM, MOE_BM)
        ys_ref[pl.ds(r0, MOE_BM), :] = jnp.zeros((MOE_BM, D_MODEL), BF16)
        return carry

    def unsort_and_plan(width):
        lax.fori_loop(n_iter * MOE_UNROLL, width // MOE_BM, clear, 0)
        pos_col = jnp.broadcast_to(posf, (LANES, tm)).T
        lane = lax.broadcasted_iota(jnp.int32, (tm, LANES), 1).astype(F32)
        pt = jnp.concatenate(
            [jnp.where(pos_col == lane + float(LANES * k), 1.0, 0.0).astype(BF16) for k in range(width // LANES)],
            axis=-1)
        y = x_ref[...] + _dot(pt, ys_ref[0:width, :])
        if final:
            y = _rmsnorm(y, gf_ref[...])
        o_ref[...] = y
        park_plan(xn_ref[...])

    fits = n_iter * (MOE_UNROLL * MOE_BM) <= MOE_R_COMMON

    @pl.when(fits)
    def _():
        unsort_and_plan(MOE_R_COMMON)

    @pl.when(jnp.logical_not(fits))
    def _():
        unsort_and_plan(MOE_R)


def _moe(x2, layer, wts, experts, g_final, *, final):
    n_tiles = x2.shape[0] // MOE_TM
    assert n_tiles * MOE_TM == x2.shape[0] and MOE_R % (MOE_BM * MOE_UNROLL) == 0
    assert MOE_TM + N_GROUPS * (MOE_BM - 1) <= MOE_R
    x_spec = pl.BlockSpec((MOE_TM, D_MODEL), lambda i: (i, 0))
    next_spec = pl.BlockSpec((MOE_TM, D_MODEL), lambda i: (jnp.minimum(i + 1, n_tiles - 1), 0))
    names = ("g_ffn", "w_r", "b_r")
    return pl.pallas_call(
        functools.partial(_moe_kernel, final=final),
        grid=(n_tiles,),
        in_specs=([x_spec, next_spec] + [_layer_spec(wts[n], layer, pipeline_mode=pl.Buffered(1)) for n in names]
                  + [pl.BlockSpec(w.shape, lambda i: (0, 0, 0), pipeline_mode=pl.Buffered(1)) for w in experts]
                  + [pl.BlockSpec(g_final.shape, lambda i: (0, 0))]),
        out_specs=x_spec,
        out_shape=jax.ShapeDtypeStruct(x2.shape, F32),
        scratch_shapes=[pltpu.VMEM((MOE_R, D_MODEL), BF16),
                        pltpu.VMEM((MOE_TM, D_MODEL), BF16),
                        pltpu.VMEM((8, MOE_TM), F32),
                        pltpu.VMEM((GATE_ROWS, MOE_TM), BF16),
                        pltpu.SMEM((N_GROUPS,), jnp.int32)],
        compiler_params=pltpu.CompilerParams(dimension_semantics=("arbitrary",), vmem_limit_bytes=VMEM_LIMIT_BYTES),
        name="moe",
    )(x2, x2, *[wts[n] for n in names], *experts, g_final)


def kernel(x_prompt, x_sample, cache_mem_k, cache_mem_v, state_pool, mem_prompt, g_mix, w_in, g_v, w_s, b_s, w_pool, s_pool, w_out, g_mem, g_xattn, w_xq, w_xk, w_xv, w_xo, g_ffn, w_group, b_group, w_router, b_router, w_gate, w_up, w_down, g_final):
    bp, seq, _ = x_prompt.shape
    bs, dseq, _ = x_sample.shape

    mem_k, mem_v, mem_kb, mem_vb, w_in_b, w_out_b, w_xq_b, w_xo_b, w_fold = _memkv(
        mem_prompt.reshape(bp * N_MEM, D_MODEL), g_mem.reshape(1, D_MODEL), w_xk, w_xv, w_in, w_out, w_xq, w_xo,
        w_pool, s_pool)
    mem_kb = mem_kb.reshape(DEPTH * bp, N_MEM, D_MODEL)
    mem_vb = mem_vb.reshape(DEPTH * bp, N_MEM, D_MODEL)

    pad_r = ROUTER_LANES - N_GROUPS - N_EXPERTS
    bias_full = jnp.repeat(jnp.swapaxes(b_s, 1, 2), HEAD_DIM_A, axis=2)
    wts = {
        "g_mix": g_mix.reshape(DEPTH, 1, D_MODEL), "w_in": w_in_b, "g_v": g_v.reshape(DEPTH, 1, WIDTH_A),
        "w_s": w_s.astype(BF16), "bias_p": bias_full,
        "coef_s": jnp.repeat(jnp.transpose(w_s[:, :, :dseq, :dseq], (0, 3, 2, 1)), HEAD_DIM_A, axis=3),
        "bias_s": bias_full[:, :dseq, :],
        "w_out": w_out_b, "w_fold": w_fold,
        "g_xattn": g_xattn.reshape(DEPTH, 1, D_MODEL), "w_xq": w_xq_b, "w_xo": w_xo_b,
        "g_ffn": g_ffn.reshape(DEPTH, 1, D_MODEL),
        "w_r": jnp.concatenate([w_group, w_router, jnp.zeros((DEPTH, D_MODEL, pad_r), F32)], axis=-1).astype(BF16),
        "b_r": jnp.concatenate([b_group, b_router, jnp.zeros((DEPTH, pad_r), F32)], axis=-1).reshape(DEPTH, 1, ROUTER_LANES),
        "w_gate": w_gate, "w_up": w_up, "w_down": w_down,
    }
    g_final2 = g_final.reshape(1, D_MODEL)

    yp, ys = x_prompt, x_sample
    cv_p, cv_s, pl_p, pl_s = [], [], [], []
    for l in range(DEPTH):
        final = l == DEPTH - 1

        yp, cvp, plp = _mix(yp, None, l, wts, s_t=1, l_t=MIX_TM, pos0=0)
        ys, cvs, pls = _mix(ys, state_pool, l, wts, s_t=SAMPLE_MIX_SEQS, l_t=dseq, pos0=PAST_LEN)
        yp, ys, *experts = _xattn(yp, ys, mem_kb, mem_vb, cache_mem_k, cache_mem_v, l, wts, l_p=PROMPT_TM)
        yp = _moe(yp.reshape(bp * seq, D_MODEL), l, wts, experts, g_final2, final=final).reshape(bp, seq, D_MODEL)
        ys = _moe(ys.reshape(bs * dseq, D_MODEL), l, wts, experts, g_final2, final=final).reshape(bs, dseq, D_MODEL)

        cv_p.append(cvp.reshape(bp, CHUNK, HEADS_A, HEAD_DIM_A))
        cv_s.append(cvs.reshape(bs, dseq, HEADS_A, HEAD_DIM_A))
        pl_p.append(plp)
        pl_s.append(pls)

    return (yp, ys, mem_k, mem_v, jnp.stack(cv_p), jnp.stack(cv_s), jnp.stack(pl_p), jnp.stack(pl_s))
```

```python
import functools

import jax
import jax.numpy as jnp
from jax import lax
from jax.experimental import pallas as pl
from jax.experimental.pallas import tpu as pltpu

D_MODEL = 1024
DEPTH = 4
PAST_LEN = 16384
CHUNK = 128
WIDTH_A = 512
WIDTH_B = 512
HEADS_A = 8
HEAD_DIM_A = 64
POOL_WINDOWS = (2, 4, 8, 16)
POOL_GROUP_DIM = 128
POOL_HIST = 15
HALO = 16
N_MEM = 256
X_HEADS = 4
X_HEAD_DIM = 256
N_GROUPS = 4
PER_GROUP = 4
N_EXPERTS = 16
D_EXPERT = 256
EPS = 1e-6
ROUTER_LANES = 128
LANES = 128
GATE_ROWS = 16

MEMKV_SEQS = 2
PROMPT_TM = 512
MIX_TM = 2048
MIX_SUB = 256
SAMPLE_MIX_SEQS = 32
MOE_TM = 512
MOE_BM = 128
MOE_R = 1024
MOE_R_COMMON = 768
MOE_UNROLL = 2

F32 = jnp.float32
BF16 = jnp.bfloat16

VMEM_LIMIT_BYTES = 56 * 1024 * 1024


def _rmsnorm(x, g):
    r = lax.rsqrt(jnp.mean(x * x, axis=-1, keepdims=True) + EPS)
    return x * r * g


def _gelu_tanh(x):
    c = 0.7978845608028654
    return 0.5 * x * (1.0 + jnp.tanh(c * (x + 0.044715 * (x * x * x))))


def _silu(x):
    return x * (1.0 / (1.0 + jnp.exp(-x)))


def _dot(a, b):
    return jnp.dot(a, b, preferred_element_type=F32)


def _dot_nt(a, b):
    return lax.dot_general(a, b, (((1,), (1,)), ((), ())), preferred_element_type=F32)


def _layer_spec(arr, layer, **kw):
    nd = arr.ndim - 1
    return pl.BlockSpec((None,) + arr.shape[1:], lambda *_: (layer,) + (0,) * nd, **kw)


_PARAMS = pltpu.CompilerParams(dimension_semantics=("arbitrary", "arbitrary"), vmem_limit_bytes=VMEM_LIMIT_BYTES)


def _memkv_kernel(mem_ref, g_ref, wk_ref, wv_ref, win_ref, wout_ref, wq_ref, wo_ref, wpool_ref, spool_ref, woutb_ref,
                  k_ref, v_ref, kb_ref, vb_ref, win_o, wout_o, wq_o, wo_o, wfold_o, wkb_ref, wvb_ref):
    win_o[...] = win_ref[...].astype(BF16)
    wout_o[...] = wout_ref[...].astype(BF16)
    wq_o[...] = wq_ref[...].astype(BF16)
    wo_o[...] = wo_ref[...].astype(BF16)
    wfold_o[0] = jnp.dot(wpool_ref[0, 0] * spool_ref[0], woutb_ref[0], preferred_element_type=F32,
                         precision=lax.Precision.HIGHEST).astype(BF16)

    @pl.when(pl.program_id(1) == 0)
    def _():
        wkb_ref[...] = wk_ref[0].astype(BF16)
        wvb_ref[...] = wv_ref[0].astype(BF16)

    h = _rmsnorm(mem_ref[...], g_ref[...]).astype(BF16)
    k = _dot(h, wkb_ref[...])
    v = _dot(h, wvb_ref[...])
    for s in range(MEMKV_SEQS):
        rows = slice(s * N_MEM, (s + 1) * N_MEM)
        for hh in range(X_HEADS):
            sl = slice(hh * X_HEAD_DIM, (hh + 1) * X_HEAD_DIM)
            k_ref[0, s, :, hh, :] = k[rows, sl]
            v_ref[0, s, :, hh, :] = v[rows, sl]
    kb_ref[0] = k.astype(BF16)
    vb_ref[0] = v.astype(BF16)


def _memkv(mem2d, g_mem, wk, wv, w_in, w_out, w_xq, w_xo, w_pool, s_pool):
    rows = mem2d.shape[0]
    tile = MEMKV_SEQS * N_MEM
    n_r = rows // tile
    slab = D_MODEL // n_r
    n_pool = len(POOL_WINDOWS)
    assert n_r * tile == rows and slab * n_r == D_MODEL and n_r == n_pool
    first_b = WIDTH_A // POOL_GROUP_DIM
    fold_in = [pl.BlockSpec((1, 1, POOL_GROUP_DIM, POOL_GROUP_DIM), lambda l, r: (l, r, 0, 0)),
               pl.BlockSpec((1, 1, POOL_GROUP_DIM), lambda l, r: (l, 0, r)),
               pl.BlockSpec((1, POOL_GROUP_DIM, D_MODEL), lambda l, r: (l, first_b + r, 0))]
    fold_out = pl.BlockSpec((1, POOL_GROUP_DIM, D_MODEL), lambda l, r: (l, r, 0))
    out_sds = jax.ShapeDtypeStruct((DEPTH, rows // N_MEM, N_MEM, X_HEADS, X_HEAD_DIM), F32)
    outb_sds = jax.ShapeDtypeStruct((DEPTH, rows, D_MODEL), BF16)
    w_spec = pl.BlockSpec((1, D_MODEL, D_MODEL), lambda l, r: (l, 0, 0))
    o5_spec = pl.BlockSpec((1, MEMKV_SEQS, N_MEM, X_HEADS, X_HEAD_DIM), lambda l, r: (l, r, 0, 0, 0))
    o_spec = pl.BlockSpec((1, tile, D_MODEL), lambda l, r: (l, r, 0))
    dense = (w_in, w_out, w_xq, w_xo)
    slab_specs = [pl.BlockSpec((1, slab, w.shape[2]), lambda l, r: (l, r, 0)) for w in dense]
    return pl.pallas_call(
        _memkv_kernel,
        grid=(DEPTH, n_r),
        in_specs=[pl.BlockSpec((tile, D_MODEL), lambda l, r: (r, 0)),
                  pl.BlockSpec((1, D_MODEL), lambda l, r: (0, 0)),
                  w_spec, w_spec] + slab_specs + fold_in,
        out_specs=[o5_spec, o5_spec, o_spec, o_spec] + slab_specs + [fold_out],
        out_shape=([out_sds, out_sds, outb_sds, outb_sds] + [jax.ShapeDtypeStruct(w.shape, BF16) for w in dense]
                   + [jax.ShapeDtypeStruct((DEPTH, WIDTH_B, D_MODEL), BF16)]),
        scratch_shapes=[pltpu.VMEM((D_MODEL, D_MODEL), BF16), pltpu.VMEM((D_MODEL, D_MODEL), BF16)],
        compiler_params=_PARAMS,
        name="memkv",
    )(mem2d, g_mem, wk, wv, *dense, w_pool, s_pool.reshape(DEPTH, 1, WIDTH_B), w_out)


def _gate_chunks(v, ws_ref, tm):
    r_i = lax.broadcasted_iota(jnp.int32, (CHUNK, CHUNK), 0)
    c_i = lax.broadcasted_iota(jnp.int32, (CHUNK, CHUNK), 1)
    causal = c_i <= r_i
    lane = lax.broadcasted_iota(jnp.int32, (CHUNK, LANES), 1)
    zero_b = jnp.zeros((), BF16)
    vb = v.astype(BF16)
    pair_w = []
    for p in range(HEADS_A // 2):
        m_lo = jnp.where(causal, ws_ref[2 * p], zero_b)
        m_hi = jnp.where(causal, ws_ref[2 * p + 1], zero_b)
        pair_w.append(jnp.concatenate([m_lo, m_hi], axis=1))
    rows = []
    for c in range(tm // CHUNK):
        cols = []
        for p in range(HEADS_A // 2):
            vp = vb[c * CHUNK:(c + 1) * CHUNK, LANES * p:LANES * (p + 1)]
            lo = jnp.where(lane < HEAD_DIM_A, vp, zero_b)
            hi = jnp.where(lane >= HEAD_DIM_A, vp, zero_b)
            cols.append(_dot(pair_w[p], jnp.concatenate([lo, hi], axis=0)))
        rows.append(jnp.concatenate(cols, axis=-1))
    return jnp.concatenate(rows, axis=0)


def _gate_short(v3, coef_ref, l_t):
    i_i = lax.broadcasted_iota(jnp.int32, (l_t, WIDTH_A), 0)
    acc = None
    for jj in range(l_t):
        cj = jnp.where(i_i >= jj, coef_ref[jj], 0.0)
        term = cj[None, :, :] * v3[:, jj:jj + 1, :]
        acc = term if acc is None else acc + term
    return acc


def _trailing_sums(zx, n_new):
    outs = []
    for g, w in enumerate(POOL_WINDOWS):
        a = zx[:, :, g * POOL_GROUP_DIM:(g + 1) * POOL_GROUP_DIM]
        span = 1
        while span < w:
            n = a.shape[1]
            a = a[:, span:, :] + a[:, :n - span, :]
            span *= 2
        first = HALO + 1 - w
        outs.append(a[:, first:first + n_new, :])
    return outs


def _mix_kernel(*refs, s_t, l_t, pos0, has_hist):
    if has_hist:
        (x_ref, hist_ref, gmix_ref, win_ref, gv_ref, ws_ref, mb_ref, wout_ref, wfold_ref,
         x1_ref, v_ref, zt_ref, zext_ref) = refs
    else:
        (x_ref, gmix_ref, win_ref, gv_ref, ws_ref, mb_ref, wout_ref, wfold_ref,
         x1_ref, v_ref, zt_ref, zext_ref, carry_ref) = refs
    j = pl.program_id(1)
    sub = min(l_t, MIX_SUB)
    ts = s_t * sub

    if has_hist:
        zext_ref[:, HALO - POOL_HIST:HALO, :] = hist_ref[...]
    else:
        @pl.when(j == 0)
        def _():
            zext_ref[:, 0:HALO, :] = jnp.zeros((s_t, HALO, WIDTH_B), F32)

        @pl.when(j > 0)
        def _():
            zext_ref[:, 0:HALO, :] = carry_ref[...]

    for r0 in range(0, l_t, sub):
        x = x_ref[:, r0:r0 + sub, :].reshape(ts, D_MODEL)
        h = _rmsnorm(x, gmix_ref[...]).astype(BF16)
        proj = _dot(h, win_ref[...])
        ua = _gelu_tanh(proj[:, :2 * WIDTH_A])
        u = ua[:, :WIDTH_A]
        v = _rmsnorm(ua[:, WIDTH_A:], gv_ref[...])
        z3 = proj[:, 2 * WIDTH_A:].reshape(s_t, sub, WIDTH_B)

        if has_hist:
            mixed = _gate_short(v.reshape(s_t, sub, WIDTH_A), ws_ref, sub) + mb_ref[...][None, :, :]
        else:
            mixed = _gate_chunks(v, ws_ref, ts).reshape(ts // CHUNK, CHUNK, WIDTH_A) + mb_ref[...][None, :, :]
        a_out = u * mixed.reshape(ts, WIDTH_A)

        zext_ref[:, HALO + r0:HALO + r0 + sub, :] = z3
        sums = _trailing_sums(zext_ref[:, r0:r0 + HALO + sub, :], sub)
        pos = pos0 + j * l_t + r0 + lax.broadcasted_iota(jnp.int32, (1, sub, 1), 1)
        pooled = []
        for g, w in enumerate(POOL_WINDOWS):
            sl = slice(g * POOL_GROUP_DIM, (g + 1) * POOL_GROUP_DIM)
            inv_cnt = 1.0 / jnp.minimum(w, pos + 1).astype(F32)
            pooled.append((sums[g] * inv_cnt - z3[:, :, sl]).reshape(ts, POOL_GROUP_DIM))

        cat = jnp.concatenate([a_out] + pooled, axis=-1).astype(BF16)
        x1 = x + _dot(cat[:, :WIDTH_A], wout_ref[0:WIDTH_A, :]) + _dot(cat[:, WIDTH_A:], wfold_ref[...])
        x1_ref[:, r0:r0 + sub, :] = x1.reshape(s_t, sub, D_MODEL)
        if has_hist:
            v_ref[...] = v.reshape(s_t, sub, WIDTH_A)
        elif r0 + sub == l_t:
            v_ref[...] = v[ts - CHUNK:, :].reshape(1, CHUNK, WIDTH_A)

    zt_ref[...] = zext_ref[:, l_t + 1:l_t + HALO, :]
    if not has_hist:
        carry_ref[...] = zext_ref[:, l_t:l_t + HALO, :]


def _mix(x3, hist, layer, wts, *, s_t, l_t, pos0):
    s_all, l_all, _ = x3.shape
    has_hist = hist is not None
    grid = (s_all // s_t, l_all // l_t)
    x_spec = pl.BlockSpec((s_t, l_t, D_MODEL), lambda b, j: (b, j, 0))
    in_specs = [x_spec]
    args = [x3]
    if has_hist:
        in_specs.append(pl.BlockSpec((None, s_t, POOL_HIST, WIDTH_B), lambda b, j: (layer, b, 0, 0)))
        args.append(hist)
    gate_w = ("coef_s", "bias_s") if has_hist else ("w_s", "bias_p")
    for name in ("g_mix", "w_in", "g_v") + gate_w + ("w_out", "w_fold"):
        in_specs.append(_layer_spec(wts[name], layer))
        args.append(wts[name])
    if has_hist:
        v_shape, v_blk = (s_all, l_all, WIDTH_A), (s_t, l_t, WIDTH_A)
    else:
        v_shape, v_blk = (s_all, CHUNK, WIDTH_A), (1, CHUNK, WIDTH_A)
    out_shape = [jax.ShapeDtypeStruct(x3.shape, F32),
                 jax.ShapeDtypeStruct(v_shape, F32),
                 jax.ShapeDtypeStruct((s_all, POOL_HIST, WIDTH_B), F32)]
    out_specs = [x_spec,
                 pl.BlockSpec(v_blk, lambda b, j: (b, 0, 0)),
                 pl.BlockSpec((s_t, POOL_HIST, WIDTH_B), lambda b, j: (b, 0, 0))]
    scratch = [pltpu.VMEM((s_t, HALO + l_t, WIDTH_B), F32)]
    if not has_hist:
        scratch.append(pltpu.VMEM((s_t, HALO, WIDTH_B), F32))
    return pl.pallas_call(
        functools.partial(_mix_kernel, s_t=s_t, l_t=l_t, pos0=pos0, has_hist=has_hist),
        grid=grid, in_specs=in_specs, out_specs=out_specs, out_shape=out_shape,
        scratch_shapes=scratch, compiler_params=_PARAMS,
        name="mix_sample" if has_hist else "mix_prompt",
    )(*args)


def _softmax_rows(sc):
    m = jnp.max(sc, axis=-1, keepdims=True)
    e = jnp.exp(sc - m)
    return e * (1.0 / jnp.sum(e, axis=-1, keepdims=True))


def _native_rows(m_ref, s):
    return m_ref[0, s].reshape(N_MEM * X_HEADS, X_HEAD_DIM).astype(BF16)


def _attend_staged(q, l_p, s_s, l_s, kp_ref, vp_ref, ks_ref, vs_ref):
    sls = [slice(hh * X_HEAD_DIM, (hh + 1) * X_HEAD_DIM) for hh in range(X_HEADS)]
    qp = q[:l_p]
    q_rows = [jnp.concatenate([q[l_p + s * l_s:l_p + (s + 1) * l_s, sl] for sl in sls], axis=0) for s in range(s_s)]
    sc_p = [_dot_nt(qp[:, sl].astype(BF16), kp_ref[0, :, sl]) for sl in sls]
    sc_s = [_dot_nt(q_rows[s].astype(BF16), _native_rows(ks_ref, s)) for s in range(s_s)]
    r_i = lax.broadcasted_iota(jnp.int32, sc_s[0].shape, 0)
    c_i = lax.broadcasted_iota(jnp.int32, sc_s[0].shape, 1)
    same_head = (r_i // l_s) == (c_i % X_HEADS)
    pr_p = [_softmax_rows(sc).astype(BF16) for sc in sc_p]
    pr_s = [_softmax_rows(jnp.where(same_head, sc, -jnp.inf)).astype(BF16) for sc in sc_s]
    outs = [jnp.concatenate([_dot(p, vp_ref[0, :, sl]) for p, sl in zip(pr_p, sls)], axis=-1)]
    for s in range(s_s):
        o_rows = _dot(pr_s[s], _native_rows(vs_ref, s))
        outs.append(jnp.concatenate([o_rows[hh * l_s:(hh + 1) * l_s] for hh in range(X_HEADS)], axis=-1))
    return jnp.concatenate(outs, axis=0)


def _xattn_kernel(xp_ref, xs_ref, kp_ref, vp_ref, ks_ref, vs_ref, g_ref, wq_ref, wo_ref, eg_ref, eu_ref, ed_ref,
                  op_ref, os_ref, eg_o, eu_o, ed_o, *, l_p, s_s, l_s):
    n_s = s_s * l_s
    x = jnp.concatenate([xp_ref[...].reshape(l_p, D_MODEL), xs_ref[...].reshape(n_s, D_MODEL)], axis=0)
    h = _rmsnorm(x, g_ref[...]).astype(BF16)
    q = _dot(h, wq_ref[...]) * (X_HEAD_DIM ** -0.5)
    o = _attend_staged(q, l_p, s_s, l_s, kp_ref, vp_ref, ks_ref, vs_ref)
    x2 = x + _dot(o.astype(BF16), wo_ref[...])
    op_ref[...] = x2[:l_p].reshape(1, l_p, D_MODEL)
    os_ref[...] = x2[l_p:].reshape(s_s, l_s, D_MODEL)
    eg_o[...] = eg_ref[...].astype(BF16)
    eu_o[...] = eu_ref[...].astype(BF16)
    ed_o[...] = ed_ref[...].astype(BF16)


def _xattn(xp, xs, mem_kb, mem_vb, cache_k, cache_v, layer, wts, *, l_p):
    bp, seq, _ = xp.shape
    s_all, l_s, _ = xs.shape
    tiles = seq // l_p
    steps = bp * tiles
    s_s = s_all // steps
    parts = steps // N_EXPERTS
    assert s_s * steps == s_all and seq % l_p == 0 and parts * N_EXPERTS == steps
    rows_in, rows_mid = D_MODEL // parts, D_EXPERT // parts

    def step(b, j):
        return b * tiles + j

    xp_spec = pl.BlockSpec((1, l_p, D_MODEL), lambda b, j: (b, j, 0))
    xs_spec = pl.BlockSpec((s_s, l_s, D_MODEL), lambda b, j: (step(b, j), 0, 0))
    mp_spec = pl.BlockSpec((1, N_MEM, D_MODEL), lambda b, j: (layer * bp + b, 0, 0))
    ms_spec = pl.BlockSpec((1, s_s, N_MEM, X_HEADS, X_HEAD_DIM), lambda b, j: (layer, step(b, j), 0, 0, 0))
    gu_in = pl.BlockSpec((None, None, rows_in, D_EXPERT),
                         lambda b, j: (layer, step(b, j) // parts, step(b, j) % parts, 0))
    dn_in = pl.BlockSpec((None, None, rows_mid, D_MODEL),
                         lambda b, j: (layer, step(b, j) // parts, step(b, j) % parts, 0))
    gu_out = pl.BlockSpec((None, rows_in, D_EXPERT),
                          lambda b, j: (step(b, j) // parts // PER_GROUP, step(b, j) % parts,
                                        step(b, j) // parts % PER_GROUP))
    dn_out = pl.BlockSpec((None, rows_mid, D_MODEL),
                          lambda b, j: (step(b, j) // parts // PER_GROUP,
                                        (step(b, j) // parts % PER_GROUP) * parts + step(b, j) % parts, 0))
    names = ("g_xattn", "w_xq", "w_xo")
    gu_sds = jax.ShapeDtypeStruct((N_GROUPS, D_MODEL, PER_GROUP * D_EXPERT), BF16)
    dn_sds = jax.ShapeDtypeStruct((N_GROUPS, PER_GROUP * D_EXPERT, D_MODEL), BF16)
    return pl.pallas_call(
        functools.partial(_xattn_kernel, l_p=l_p, s_s=s_s, l_s=l_s),
        grid=(bp, tiles),
        in_specs=([xp_spec, xs_spec, mp_spec, mp_spec, ms_spec, ms_spec] + [_layer_spec(wts[n], layer) for n in names]
                  + [gu_in, gu_in, dn_in]),
        out_specs=[xp_spec, xs_spec, gu_out, gu_out, dn_out],
        out_shape=[jax.ShapeDtypeStruct(xp.shape, F32), jax.ShapeDtypeStruct(xs.shape, F32), gu_sds, gu_sds, dn_sds],
        compiler_params=_PARAMS,
        name="xattn",
    )(xp, xs, mem_kb, mem_vb, cache_k, cache_v, *[wts[n] for n in names],
      wts["w_gate"], wts["w_up"], wts["w_down"])


def _first_index(vals, target):
    idx = jnp.full(target.shape, len(vals) - 1, jnp.int32)
    for i in range(len(vals) - 2, -1, -1):
        idx = jnp.where(vals[i] == target, i, idx)
    return idx


def _max_of(vals):
    m = vals[0]
    for v in vals[1:]:
        m = jnp.maximum(m, v)
    return m


def _route(lt):
    rows = [lt[i:i + 1, :] for i in range(N_GROUPS + N_EXPERTS)]
    g_l = rows[:N_GROUPS]
    gmax = _max_of(g_l)
    gsum = g_l[0] * 0.0
    for v in g_l:
        gsum = gsum + jnp.exp(v - gmax)
    g_p = 1.0 / gsum
    g_idx = _first_index(g_l, gmax)
    e_sel = []
    for j in range(PER_GROUP):
        v = rows[N_GROUPS + PER_GROUP * (N_GROUPS - 1) + j]
        for g in range(N_GROUPS - 2, -1, -1):
            v = jnp.where(g_idx == g, rows[N_GROUPS + PER_GROUP * g + j], v)
        e_sel.append(v)
    v1 = _max_of(e_sel)
    i1 = _first_index(e_sel, v1)
    neg = jnp.float32(-jnp.inf)
    e_rest = [jnp.where(i1 == j, neg, e_sel[j]) for j in range(PER_GROUP)]
    v2 = _max_of(e_rest)
    i2 = _first_index(e_rest, v2)
    t = jnp.exp(v2 - v1)
    w1 = g_p / (1.0 + t)
    w2 = g_p * t / (1.0 + t)
    gates = [jnp.where(i1 == j, w1, 0.0) + jnp.where(i2 == j, w2, 0.0) for j in range(PER_GROUP)]
    return g_idx, gates


def _stack_rows(rows, n_rows, tm):
    ri = lax.broadcasted_iota(jnp.int32, (n_rows, tm), 0)
    out = jnp.zeros((n_rows, tm), F32)
    for i, r in enumerate(rows):
        out = jnp.where(ri == i, r, out)
    return out


def _moe_plan(x, g_ref, wr_ref, br_ref):
    tm = MOE_TM
    h = _rmsnorm(x, g_ref[...]).astype(BF16)
    logits = _dot(h, wr_ref[...]) + br_ref[...]
    g_idx, gates = _route(logits.T)

    member = [(g_idx == g).astype(F32) for g in range(N_GROUPS)]
    t_r = lax.broadcasted_iota(jnp.int32, (tm, tm), 0)
    t_c = lax.broadcasted_iota(jnp.int32, (tm, tm), 1)
    upper = jnp.where(t_r <= t_c, 1.0, 0.0).astype(BF16)
    incl = _dot(_stack_rows(member, GATE_ROWS, tm).astype(BF16), upper)
    end_blk = []
    posf = jnp.zeros((1, tm), F32)
    run = jnp.zeros((1, 1), F32)
    for g in range(N_GROUPS):
        n_g = jnp.sum(member[g], axis=-1, keepdims=True)
        posf = posf + member[g] * (run * MOE_BM + incl[g:g + 1, :] - 1.0)
        run = run + jnp.floor((n_g + (MOE_BM - 1)) * (1.0 / MOE_BM))
        end_blk.append(run)

    g_hi = [gt.astype(BF16).astype(F32) for gt in gates]
    g_lo = [gt - hi for gt, hi in zip(gates, g_hi)]
    gmat = _stack_rows(g_hi + g_lo, GATE_ROWS, tm).astype(BF16)
    return h, posf, gmat, end_blk


def _moe_kernel(x_ref, xn_ref, g_ref, wr_ref, br_ref, wg_ref, wu_ref, wd_ref, gf_ref, o_ref,
                ys_ref, h_ref, pos_ref, gmat_ref, cnt_ref, *, final):
    tm = MOE_TM

    def park_plan(x):
        h, posf, gmat, end_blk = _moe_plan(x, g_ref, wr_ref, br_ref)
        h_ref[...] = h
        pos_ref[...] = jnp.broadcast_to(posf, pos_ref.shape)
        gmat_ref[...] = gmat
        for k, e in enumerate(end_blk):
            cnt_ref[k] = e.astype(jnp.int32)[0, 0]

    @pl.when(pl.program_id(0) == 0)
    def _():
        park_plan(x_ref[...])

    posf = pos_ref[0:1, :]
    pos = posf.astype(jnp.int32)
    gmat = gmat_ref[...]
    e0, e1, e2, n_blk = [cnt_ref[k] for k in range(N_GROUPS)]

    r_iota = lax.broadcasted_iota(jnp.int32, (MOE_BM, tm), 0)

    def block(b):
        r0 = pl.multiple_of(b * MOE_BM, MOE_BM)
        grp = (b >= e0).astype(jnp.int32) + (b >= e1).astype(jnp.int32) + (b >= e2).astype(jnp.int32)
        pb = jnp.where(r_iota + r0 == pos, 1.0, 0.0).astype(BF16)
        xb = _dot(pb, h_ref[...]).astype(BF16)
        gs = _dot_nt(pb, gmat)
        hg = _dot(xb, wg_ref[grp])
        hu = _dot(xb, wu_ref[grp])
        gate_cols = jnp.concatenate(
            [jnp.broadcast_to(gs[:, j:j + 1] + gs[:, PER_GROUP + j:PER_GROUP + j + 1], (MOE_BM, D_EXPERT))
             for j in range(PER_GROUP)], axis=-1)
        a = (_silu(hg) * hu * gate_cols).astype(BF16)
        ys_ref[pl.ds(r0, MOE_BM), :] = _dot(a, wd_ref[grp]).astype(BF16)

    n_iter = lax.shift_right_logical(n_blk + (MOE_UNROLL - 1), MOE_UNROLL.bit_length() - 1)

    def blocks(i, carry):
        for u in range(MOE_UNROLL):
            block(i * MOE_UNROLL + u)
        return carry

    lax.fori_loop(0, n_iter, blocks, 0)

    def clear(b, carry):
        r0 = pl.multiple_of(b * MOE_BM, MOE_BM)
        ys_ref[pl.ds(r0, MOE_BM), :] = jnp.zeros((MOE_BM, D_MODEL), BF16)
        return carry

    def unsort_and_plan(width):
        lax.fori_loop(n_iter * MOE_UNROLL, width // MOE_BM, clear, 0)
        pos_col = jnp.broadcast_to(posf, (LANES, tm)).T
        lane = lax.broadcasted_iota(jnp.int32, (tm, LANES), 1).astype(F32)
        pt = jnp.concatenate(
            [jnp.where(pos_col == lane + float(LANES * k), 1.0, 0.0).astype(BF16) for k in range(width // LANES)],
            axis=-1)
        y = x_ref[...] + _dot(pt, ys_ref[0:width, :])
        if final:
            y = _rmsnorm(y, gf_ref[...])
        o_ref[...] = y
        park_plan(xn_ref[...])

    fits = n_iter * (MOE_UNROLL * MOE_BM) <= MOE_R_COMMON

    @pl.when(fits)
    def _():
        unsort_and_plan(MOE_R_COMMON)

    @pl.when(jnp.logical_not(fits))
    def _():
        unsort_and_plan(MOE_R)


def _moe(x2, layer, wts, experts, g_final, *, final):
    n_tiles = x2.shape[0] // MOE_TM
    assert n_tiles * MOE_TM == x2.shape[0] and MOE_R % (MOE_BM * MOE_UNROLL) == 0
    assert MOE_TM + N_GROUPS * (MOE_BM - 1) <= MOE_R
    x_spec = pl.BlockSpec((MOE_TM, D_MODEL), lambda i: (i, 0))
    next_spec = pl.BlockSpec((MOE_TM, D_MODEL), lambda i: (jnp.minimum(i + 1, n_tiles - 1), 0))
    names = ("g_ffn", "w_r", "b_r")
    return pl.pallas_call(
        functools.partial(_moe_kernel, final=final),
        grid=(n_tiles,),
        in_specs=([x_spec, next_spec] + [_layer_spec(wts[n], layer, pipeline_mode=pl.Buffered(1)) for n in names]
                  + [pl.BlockSpec(w.shape, lambda i: (0, 0, 0), pipeline_mode=pl.Buffered(1)) for w in experts]
                  + [pl.BlockSpec(g_final.shape, lambda i: (0, 0))]),
        out_specs=x_spec,
        out_shape=jax.ShapeDtypeStruct(x2.shape, F32),
        scratch_shapes=[pltpu.VMEM((MOE_R, D_MODEL), BF16),
                        pltpu.VMEM((MOE_TM, D_MODEL), BF16),
                        pltpu.VMEM((8, MOE_TM), F32),
                        pltpu.VMEM((GATE_ROWS, MOE_TM), BF16),
                        pltpu.SMEM((N_GROUPS,), jnp.int32)],
        compiler_params=pltpu.CompilerParams(dimension_semantics=("arbitrary",), vmem_limit_bytes=VMEM_LIMIT_BYTES),
        name="moe",
    )(x2, x2, *[wts[n] for n in names], *experts, g_final)


def kernel(x_prompt, x_sample, cache_mem_k, cache_mem_v, state_pool, mem_prompt, g_mix, w_in, g_v, w_s, b_s, w_pool, s_pool, w_out, g_mem, g_xattn, w_xq, w_xk, w_xv, w_xo, g_ffn, w_group, b_group, w_router, b_router, w_gate, w_up, w_down, g_final):
    bp, seq, _ = x_prompt.shape
    bs, dseq, _ = x_sample.shape

    mem_k, mem_v, mem_kb, mem_vb, w_in_b, w_out_b, w_xq_b, w_xo_b, w_fold = _memkv(
        mem_prompt.reshape(bp * N_MEM, D_MODEL), g_mem.reshape(1, D_MODEL), w_xk, w_xv, w_in, w_out, w_xq, w_xo,
        w_pool, s_pool)
    mem_kb = mem_kb.reshape(DEPTH * bp, N_MEM, D_MODEL)
    mem_vb = mem_vb.reshape(DEPTH * bp, N_MEM, D_MODEL)

    pad_r = ROUTER_LANES - N_GROUPS - N_EXPERTS
    bias_full = jnp.repeat(jnp.swapaxes(b_s, 1, 2), HEAD_DIM_A, axis=2)
    wts = {
        "g_mix": g_mix.reshape(DEPTH, 1, D_MODEL), "w_in": w_in_b, "g_v": g_v.reshape(DEPTH, 1, WIDTH_A),
        "w_s": w_s.astype(BF16), "bias_p": bias_full,
        "coef_s": jnp.repeat(jnp.transpose(w_s[:, :, :dseq, :dseq], (0, 3, 2, 1)), HEAD_DIM_A, axis=3),
        "bias_s": bias_full[:, :dseq, :],
        "w_out": w_out_b, "w_fold": w_fold,
        "g_xattn": g_xattn.reshape(DEPTH, 1, D_MODEL), "w_xq": w_xq_b, "w_xo": w_xo_b,
        "g_ffn": g_ffn.reshape(DEPTH, 1, D_MODEL),
        "w_r": jnp.concatenate([w_group, w_router, jnp.zeros((DEPTH, D_MODEL, pad_r), F32)], axis=-1).astype(BF16),
        "b_r": jnp.concatenate([b_group, b_router, jnp.zeros((DEPTH, pad_r), F32)], axis=-1).reshape(DEPTH, 1, ROUTER_LANES),
        "w_gate": w_gate, "w_up": w_up, "w_down": w_down,
    }
    g_final2 = g_final.reshape(1, D_MODEL)

    yp, ys = x_prompt, x_sample
    cv_p, cv_s, pl_p, pl_s = [], [], [], []
    for l in range(DEPTH):
        final = l == DEPTH - 1

        yp, cvp, plp = _mix(yp, None, l, wts, s_t=1, l_t=MIX_TM, pos0=0)
        ys, cvs, pls = _mix(ys, state_pool, l, wts, s_t=SAMPLE_MIX_SEQS, l_t=dseq, pos0=PAST_LEN)
        yp, ys, *experts = _xattn(yp, ys, mem_kb, mem_vb, cache_mem_k, cache_mem_v, l, wts, l_p=PROMPT_TM)
        yp = _moe(yp.reshape(bp * seq, D_MODEL), l, wts, experts, g_final2, final=final).reshape(bp, seq, D_MODEL)
        ys = _moe(ys.reshape(bs * dseq, D_MODEL), l, wts, experts, g_final2, final=final).reshape(bs, dseq, D_MODEL)

        cv_p.append(cvp.reshape(bp, CHUNK, HEADS_A, HEAD_DIM_A))
        cv_s.append(cvs.reshape(bs, dseq, HEADS_A, HEAD_DIM_A))
        pl_p.append(plp)
        pl_s.append(pls)

    return (yp, ys, mem_k, mem_v, jnp.stack(cv_p), jnp.stack(cv_s), jnp.stack(pl_p), jnp.stack(pl_s))
```

```python
import functools

import jax
import jax.numpy as jnp
from jax import lax
from jax.experimental import pallas as pl
from jax.experimental.pallas import tpu as pltpu

D_MODEL = 1024
DEPTH = 4
PAST_LEN = 16384
CHUNK = 128
WIDTH_A = 512
WIDTH_B = 512
HEADS_A = 8
HEAD_DIM_A = 64
POOL_WINDOWS = (2, 4, 8, 16)
POOL_GROUP_DIM = 128
POOL_HIST = 15
HALO = 16
N_MEM = 256
X_HEADS = 4
X_HEAD_DIM = 256
N_GROUPS = 4
PER_GROUP = 4
N_EXPERTS = 16
D_EXPERT = 256
EPS = 1e-6
ROUTER_LANES = 128
LANES = 128
GATE_ROWS = 16

MEMKV_SEQS = 2
PROMPT_TM = 512
MIX_TM = 2048
MIX_SUB = 256
SAMPLE_MIX_SEQS = 32
MOE_TM = 512
MOE_BM = 128
MOE_R = 1024
MOE_R_COMMON = 768
MOE_UNROLL = 2

F32 = jnp.float32
BF16 = jnp.bfloat16

VMEM_LIMIT_BYTES = 56 * 1024 * 1024


def _rmsnorm(x, g):
    r = lax.rsqrt(jnp.mean(x * x, axis=-1, keepdims=True) + EPS)
    return x * r * g


def _gelu_tanh(x):
    c = 0.7978845608028654
    return 0.5 * x * (1.0 + jnp.tanh(c * (x + 0.044715 * (x * x * x))))


def _silu(x):
    return x * (1.0 / (1.0 + jnp.exp(-x)))


def _dot(a, b):
    return jnp.dot(a, b, preferred_element_type=F32)


def _dot_nt(a, b):
    return lax.dot_general(a, b, (((1,), (1,)), ((), ())), preferred_element_type=F32)


def _layer_spec(arr, layer, **kw):
    nd = arr.ndim - 1
    return pl.BlockSpec((None,) + arr.shape[1:], lambda *_: (layer,) + (0,) * nd, **kw)


_PARAMS = pltpu.CompilerParams(dimension_semantics=("arbitrary", "arbitrary"), vmem_limit_bytes=VMEM_LIMIT_BYTES)


def _memkv_kernel(mem_ref, g_ref, wk_ref, wv_ref, win_ref, wout_ref, wq_ref, wo_ref, wpool_ref, spool_ref, woutb_ref,
                  k_ref, v_ref, kb_ref, vb_ref, win_o, wout_o, wq_o, wo_o, wfold_o, wkb_ref, wvb_ref):
    win_o[...] = win_ref[...].astype(BF16)
    wout_o[...] = wout_ref[...].astype(BF16)
    wq_o[...] = wq_ref[...].astype(BF16)
    wo_o[...] = wo_ref[...].astype(BF16)
    wfold_o[0] = jnp.dot(wpool_ref[0, 0] * spool_ref[0], woutb_ref[0], preferred_element_type=F32,
                         precision=lax.Precision.HIGHEST).astype(BF16)

    @pl.when(pl.program_id(1) == 0)
    def _():
        wkb_ref[...] = wk_ref[0].astype(BF16)
        wvb_ref[...] = wv_ref[0].astype(BF16)

    h = _rmsnorm(mem_ref[...], g_ref[...]).astype(BF16)
    k = _dot(h, wkb_ref[...])
    v = _dot(h, wvb_ref[...])
    for s in range(MEMKV_SEQS):
        rows = slice(s * N_MEM, (s + 1) * N_MEM)
        for hh in range(X_HEADS):
            sl = slice(hh * X_HEAD_DIM, (hh + 1) * X_HEAD_DIM)
            k_ref[0, s, :, hh, :] = k[rows, sl]
            v_ref[0, s, :, hh, :] = v[rows, sl]
    kb_ref[0] = k.astype(BF16)
    vb_ref[0] = v.astype(BF16)


def _memkv(mem2d, g_mem, wk, wv, w_in, w_out, w_xq, w_xo, w_pool, s_pool):
    rows = mem2d.shape[0]
    tile = MEMKV_SEQS * N_MEM
    n_r = rows // tile
    slab = D_MODEL // n_r
    n_pool = len(POOL_WINDOWS)
    assert n_r * tile == rows and slab * n_r == D_MODEL and n_r == n_pool
    first_b = WIDTH_A // POOL_GROUP_DIM
    fold_in = [pl.BlockSpec((1, 1, POOL_GROUP_DIM, POOL_GROUP_DIM), lambda l, r: (l, r, 0, 0)),
               pl.BlockSpec((1, 1, POOL_GROUP_DIM), lambda l, r: (l, 0, r)),
               pl.BlockSpec((1, POOL_GROUP_DIM, D_MODEL), lambda l, r: (l, first_b + r, 0))]
    fold_out = pl.BlockSpec((1, POOL_GROUP_DIM, D_MODEL), lambda l, r: (l, r, 0))
    out_sds = jax.ShapeDtypeStruct((DEPTH, rows // N_MEM, N_MEM, X_HEADS, X_HEAD_DIM), F32)
    outb_sds = jax.ShapeDtypeStruct((DEPTH, rows, D_MODEL), BF16)
    w_spec = pl.BlockSpec((1, D_MODEL, D_MODEL), lambda l, r: (l, 0, 0))
    o5_spec = pl.BlockSpec((1, MEMKV_SEQS, N_MEM, X_HEADS, X_HEAD_DIM), lambda l, r: (l, r, 0, 0, 0))
    o_spec = pl.BlockSpec((1, tile, D_MODEL), lambda l, r: (l, r, 0))
    dense = (w_in, w_out, w_xq, w_xo)
    slab_specs = [pl.BlockSpec((1, slab, w.shape[2]), lambda l, r: (l, r, 0)) for w in dense]
    return pl.pallas_call(
        _memkv_kernel,
        grid=(DEPTH, n_r),
        in_specs=[pl.BlockSpec((tile, D_MODEL), lambda l, r: (r, 0)),
                  pl.BlockSpec((1, D_MODEL), lambda l, r: (0, 0)),
                  w_spec, w_spec] + slab_specs + fold_in,
        out_specs=[o5_spec, o5_spec, o_spec, o_spec] + slab_specs + [fold_out],
        out_shape=([out_sds, out_sds, outb_sds, outb_sds] + [jax.ShapeDtypeStruct(w.shape, BF16) for w in dense]
                   + [jax.ShapeDtypeStruct((DEPTH, WIDTH_B, D_MODEL), BF16)]),
        scratch_shapes=[pltpu.VMEM((D_MODEL, D_MODEL), BF16), pltpu.VMEM((D_MODEL, D_MODEL), BF16)],
        compiler_params=_PARAMS,
        name="memkv",
    )(mem2d, g_mem, wk, wv, *dense, w_pool, s_pool.reshape(DEPTH, 1, WIDTH_B), w_out)


def _gate_chunks(v, ws_ref, tm):
    r_i = lax.broadcasted_iota(jnp.int32, (CHUNK, CHUNK), 0)
    c_i = lax.broadcasted_iota(jnp.int32, (CHUNK, CHUNK), 1)
    causal = c_i <= r_i
    lane = lax.broadcasted_iota(jnp.int32, (CHUNK, LANES), 1)
    zero_b = jnp.zeros((), BF16)
    vb = v.astype(BF16)
    pair_w = []
    for p in range(HEADS_A // 2):
        m_lo = jnp.where(causal, ws_ref[2 * p], zero_b)
        m_hi = jnp.where(causal, ws_ref[2 * p + 1], zero_b)
        pair_w.append(jnp.concatenate([m_lo, m_hi], axis=1))
    rows = []
    for c in range(tm // CHUNK):
        cols = []
        for p in range(HEADS_A // 2):
            vp = vb[c * CHUNK:(c + 1) * CHUNK, LANES * p:LANES * (p + 1)]
            lo = jnp.where(lane < HEAD_DIM_A, vp, zero_b)
            hi = jnp.where(lane >= HEAD_DIM_A, vp, zero_b)
            cols.append(_dot(pair_w[p], jnp.concatenate([lo, hi], axis=0)))
        rows.append(jnp.concatenate(cols, axis=-1))
    return jnp.concatenate(rows, axis=0)


def _gate_short(v3, coef_ref, l_t):
    i_i = lax.broadcasted_iota(jnp.int32, (l_t, WIDTH_A), 0)
    acc = None
    for jj in range(l_t):
        cj = jnp.where(i_i >= jj, coef_ref[jj], 0.0)
        term = cj[None, :, :] * v3[:, jj:jj + 1, :]
        acc = term if acc is None else acc + term
    return acc


def _trailing_sums(zx, n_new):
    outs = []
    for g, w in enumerate(POOL_WINDOWS):
        a = zx[:, :, g * POOL_GROUP_DIM:(g + 1) * POOL_GROUP_DIM]
        span = 1
        while span < w:
            n = a.shape[1]
            a = a[:, span:, :] + a[:, :n - span, :]
            span *= 2
        first = HALO + 1 - w
        outs.append(a[:, first:first + n_new, :])
    return outs


def _mix_kernel(*refs, s_t, l_t, pos0, has_hist):
    if has_hist:
        (x_ref, hist_ref, gmix_ref, win_ref, gv_ref, ws_ref, mb_ref, wout_ref, wfold_ref,
         x1_ref, v_ref, zt_ref, zext_ref) = refs
    else:
        (x_ref, gmix_ref, win_ref, gv_ref, ws_ref, mb_ref, wout_ref, wfold_ref,
         x1_ref, v_ref, zt_ref, zext_ref, carry_ref) = refs
    j = pl.program_id(1)
    sub = min(l_t, MIX_SUB)
    ts = s_t * sub

    if has_hist:
        zext_ref[:, HALO - POOL_HIST:HALO, :] = hist_ref[...]
    else:
        @pl.when(j == 0)
        def _():
            zext_ref[:, 0:HALO, :] = jnp.zeros((s_t, HALO, WIDTH_B), F32)

        @pl.when(j > 0)
        def _():
            zext_ref[:, 0:HALO, :] = carry_ref[...]

    for r0 in range(0, l_t, sub):
        x = x_ref[:, r0:r0 + sub, :].reshape(ts, D_MODEL)
        h = _rmsnorm(x, gmix_ref[...]).astype(BF16)
        proj = _dot(h, win_ref[...])
        ua = _gelu_tanh(proj[:, :2 * WIDTH_A])
        u = ua[:, :WIDTH_A]
        v = _rmsnorm(ua[:, WIDTH_A:], gv_ref[...])
        z3 = proj[:, 2 * WIDTH_A:].reshape(s_t, sub, WIDTH_B)

        if has_hist:
            mixed = _gate_short(v.reshape(s_t, sub, WIDTH_A), ws_ref, sub) + mb_ref[...][None, :, :]
        else:
            mixed = _gate_chunks(v, ws_ref, ts).reshape(ts // CHUNK, CHUNK, WIDTH_A) + mb_ref[...][None, :, :]
        a_out = u * mixed.reshape(ts, WIDTH_A)

        zext_ref[:, HALO + r0:HALO + r0 + sub, :] = z3
        sums = _trailing_sums(zext_ref[:, r0:r0 + HALO + sub, :], sub)
        pos = pos0 + j * l_t + r0 + lax.broadcasted_iota(jnp.int32, (1, sub, 1), 1)
        pooled = []
        for g, w in enumerate(POOL_WINDOWS):
            sl = slice(g * POOL_GROUP_DIM, (g + 1) * POOL_GROUP_DIM)
            inv_cnt = 1.0 / jnp.minimum(w, pos + 1).astype(F32)
            pooled.append((sums[g] * inv_cnt - z3[:, :, sl]).reshape(ts, POOL_GROUP_DIM))

        cat = jnp.concatenate([a_out] + pooled, axis=-1).astype(BF16)
        x1 = x + _dot(cat[:, :WIDTH_A], wout_ref[0:WIDTH_A, :]) + _dot(cat[:, WIDTH_A:], wfold_ref[...])
        x1_ref[:, r0:r0 + sub, :] = x1.reshape(s_t, sub, D_MODEL)
        if has_hist:
            v_ref[...] = v.reshape(s_t, sub, WIDTH_A)
        elif r0 + sub == l_t:
            v_ref[...] = v[ts - CHUNK:, :].reshape(1, CHUNK, WIDTH_A)

    zt_ref[...] = zext_ref[:, l_t + 1:l_t + HALO, :]
    if not has_hist:
        carry_ref[...] = zext_ref[:, l_t:l_t + HALO, :]


def _mix(x3, hist, layer, wts, *, s_t, l_t, pos0):
    s_all, l_all, _ = x3.shape
    has_hist = hist is not None
    grid = (s_all // s_t, l_all // l_t)
    x_spec = pl.BlockSpec((s_t, l_t, D_MODEL), lambda b, j: (b, j, 0))
    in_specs = [x_spec]
    args = [x3]
    if has_hist:
        in_specs.append(pl.BlockSpec((None, s_t, POOL_HIST, WIDTH_B), lambda b, j: (layer, b, 0, 0)))
        args.append(hist)
    gate_w = ("coef_s", "bias_s") if has_hist else ("w_s", "bias_p")
    for name in ("g_mix", "w_in", "g_v") + gate_w + ("w_out", "w_fold"):
        in_specs.append(_layer_spec(wts[name], layer))
        args.append(wts[name])
    if has_hist:
        v_shape, v_blk = (s_all, l_all, WIDTH_A), (s_t, l_t, WIDTH_A)
    else:
        v_shape, v_blk = (s_all, CHUNK, WIDTH_A), (1, CHUNK, WIDTH_A)
    out_shape = [jax.ShapeDtypeStruct(x3.shape, F32),
                 jax.ShapeDtypeStruct(v_shape, F32),
                 jax.ShapeDtypeStruct((s_all, POOL_HIST, WIDTH_B), F32)]
    out_specs = [x_spec,
                 pl.BlockSpec(v_blk, lambda b, j: (b, 0, 0)),
                 pl.BlockSpec((s_t, POOL_HIST, WIDTH_B), lambda b, j: (b, 0, 0))]
    scratch = [pltpu.VMEM((s_t, HALO + l_t, WIDTH_B), F32)]
    if not has_hist:
        scratch.append(pltpu.VMEM((s_t, HALO, WIDTH_B), F32))
    return pl.pallas_call(
        functools.partial(_mix_kernel, s_t=s_t, l_t=l_t, pos0=pos0, has_hist=has_hist),
        grid=grid, in_specs=in_specs, out_specs=out_specs, out_shape=out_shape,
        scratch_shapes=scratch, compiler_params=_PARAMS,
        name="mix_sample" if has_hist else "mix_prompt",
    )(*args)


def _softmax_rows(sc):
    m = jnp.max(sc, axis=-1, keepdims=True)
    e = jnp.exp(sc - m)
    return e * (1.0 / jnp.sum(e, axis=-1, keepdims=True))


def _native_rows(m_ref, s):
    return m_ref[0, s].reshape(N_MEM * X_HEADS, X_HEAD_DIM).astype(BF16)


def _attend_staged(q, l_p, s_s, l_s, kp_ref, vp_ref, ks_ref, vs_ref):
    sls = [slice(hh * X_HEAD_DIM, (hh + 1) * X_HEAD_DIM) for hh in range(X_HEADS)]
    qp = q[:l_p]
    q_rows = [jnp.concatenate([q[l_p + s * l_s:l_p + (s + 1) * l_s, sl] for sl in sls], axis=0) for s in range(s_s)]
    sc_p = [_dot_nt(qp[:, sl].astype(BF16), kp_ref[0, :, sl]) for sl in sls]
    sc_s = [_dot_nt(q_rows[s].astype(BF16), _native_rows(ks_ref, s)) for s in range(s_s)]
    r_i = lax.broadcasted_iota(jnp.int32, sc_s[0].shape, 0)
    c_i = lax.broadcasted_iota(jnp.int32, sc_s[0].shape, 1)
    same_head = (r_i // l_s) == (c_i % X_HEADS)
    ex_p = [jnp.exp(sc - jnp.max(sc, axis=-1, keepdims=True)) for sc in sc_p]
    ex_s = [jnp.exp(sc - jnp.max(sc, axis=-1, keepdims=True))
            for sc in (jnp.where(same_head, sc, -jnp.inf) for sc in sc_s)]
    outs = [jnp.concatenate([_dot(e.astype(BF16), vp_ref[0, :, sl]) * (1.0 / jnp.sum(e, axis=-1, keepdims=True))
                             for e, sl in zip(ex_p, sls)], axis=-1)]
    for s in range(s_s):
        o_rows = _dot(ex_s[s].astype(BF16), _native_rows(vs_ref, s)) * (1.0 / jnp.sum(ex_s[s], axis=-1, keepdims=True))
        outs.append(jnp.concatenate([o_rows[hh * l_s:(hh + 1) * l_s] for hh in range(X_HEADS)], axis=-1))
    return jnp.concatenate(outs, axis=0)


def _xattn_kernel(xp_ref, xs_ref, kp_ref, vp_ref, ks_ref, vs_ref, g_ref, wq_ref, wo_ref, eg_ref, eu_ref, ed_ref,
                  op_ref, os_ref, eg_o, eu_o, ed_o, *, l_p, s_s, l_s):
    n_s = s_s * l_s
    x = jnp.concatenate([xp_ref[...].reshape(l_p, D_MODEL), xs_ref[...].reshape(n_s, D_MODEL)], axis=0)
    h = _rmsnorm(x, g_ref[...]).astype(BF16)
    q = _dot(h, wq_ref[...]) * (X_HEAD_DIM ** -0.5)
    o = _attend_staged(q, l_p, s_s, l_s, kp_ref, vp_ref, ks_ref, vs_ref)
    x2 = x + _dot(o.astype(BF16), wo_ref[...])
    op_ref[...] = x2[:l_p].reshape(1, l_p, D_MODEL)
    os_ref[...] = x2[l_p:].reshape(s_s, l_s, D_MODEL)
    eg_o[...] = eg_ref[...].astype(BF16)
    eu_o[...] = eu_ref[...].astype(BF16)
    ed_o[...] = ed_ref[...].astype(BF16)


def _xattn(xp, xs, mem_kb, mem_vb, cache_k, cache_v, layer, wts, *, l_p):
    bp, seq, _ = xp.shape
    s_all, l_s, _ = xs.shape
    tiles = seq // l_p
    steps = bp * tiles
    s_s = s_all // steps
    parts = steps // N_EXPERTS
    assert s_s * steps == s_all and seq % l_p == 0 and parts * N_EXPERTS == steps
    rows_in, rows_mid = D_MODEL // parts, D_EXPERT // parts

    def step(b, j):
        return b * tiles + j

    xp_spec = pl.BlockSpec((1, l_p, D_MODEL), lambda b, j: (b, j, 0))
    xs_spec = pl.BlockSpec((s_s, l_s, D_MODEL), lambda b, j: (step(b, j), 0, 0))
    mp_spec = pl.BlockSpec((1, N_MEM, D_MODEL), lambda b, j: (layer * bp + b, 0, 0))
    ms_spec = pl.BlockSpec((1, s_s, N_MEM, X_HEADS, X_HEAD_DIM), lambda b, j: (layer, step(b, j), 0, 0, 0))
    gu_in = pl.BlockSpec((None, None, rows_in, D_EXPERT),
                         lambda b, j: (layer, step(b, j) // parts, step(b, j) % parts, 0))
    dn_in = pl.BlockSpec((None, None, rows_mid, D_MODEL),
                         lambda b, j: (layer, step(b, j) // parts, step(b, j) % parts, 0))
    gu_out = pl.BlockSpec((None, rows_in, D_EXPERT),
                          lambda b, j: (step(b, j) // parts // PER_GROUP, step(b, j) % parts,
                                        step(b, j) // parts % PER_GROUP))
    dn_out = pl.BlockSpec((None, rows_mid, D_MODEL),
                          lambda b, j: (step(b, j) // parts // PER_GROUP,
                                        (step(b, j) // parts % PER_GROUP) * parts + step(b, j) % parts, 0))
    names = ("g_xattn", "w_xq", "w_xo")
    gu_sds = jax.ShapeDtypeStruct((N_GROUPS, D_MODEL, PER_GROUP * D_EXPERT), BF16)
    dn_sds = jax.ShapeDtypeStruct((N_GROUPS, PER_GROUP * D_EXPERT, D_MODEL), BF16)
    return pl.pallas_call(
        functools.partial(_xattn_kernel, l_p=l_p, s_s=s_s, l_s=l_s),
        grid=(bp, tiles),
        in_specs=([xp_spec, xs_spec, mp_spec, mp_spec, ms_spec, ms_spec] + [_layer_spec(wts[n], layer) for n in names]
                  + [gu_in, gu_in, dn_in]),
        out_specs=[xp_spec, xs_spec, gu_out, gu_out, dn_out],
        out_shape=[jax.ShapeDtypeStruct(xp.shape, F32), jax.ShapeDtypeStruct(xs.shape, F32), gu_sds, gu_sds, dn_sds],
        compiler_params=_PARAMS,
        name="xattn",
    )(xp, xs, mem_kb, mem_vb, cache_k, cache_v, *[wts[n] for n in names],
      wts["w_gate"], wts["w_up"], wts["w_down"])


def _first_index(vals, target):
    idx = jnp.full(target.shape, len(vals) - 1, jnp.int32)
    for i in range(len(vals) - 2, -1, -1):
        idx = jnp.where(vals[i] == target, i, idx)
    return idx


def _max_of(vals):
    m = vals[0]
    for v in vals[1:]:
        m = jnp.maximum(m, v)
    return m


def _route(lt):
    rows = [lt[i:i + 1, :] for i in range(N_GROUPS + N_EXPERTS)]
    g_l = rows[:N_GROUPS]
    gmax = _max_of(g_l)
    gsum = g_l[0] * 0.0
    for v in g_l:
        gsum = gsum + jnp.exp(v - gmax)
    g_p = 1.0 / gsum
    g_idx = _first_index(g_l, gmax)
    e_sel = []
    for j in range(PER_GROUP):
        v = rows[N_GROUPS + PER_GROUP * (N_GROUPS - 1) + j]
        for g in range(N_GROUPS - 2, -1, -1):
            v = jnp.where(g_idx == g, rows[N_GROUPS + PER_GROUP * g + j], v)
        e_sel.append(v)
    v1 = _max_of(e_sel)
    i1 = _first_index(e_sel, v1)
    neg = jnp.float32(-jnp.inf)
    e_rest = [jnp.where(i1 == j, neg, e_sel[j]) for j in range(PER_GROUP)]
    v2 = _max_of(e_rest)
    i2 = _first_index(e_rest, v2)
    t = jnp.exp(v2 - v1)
    w1 = g_p / (1.0 + t)
    w2 = g_p * t / (1.0 + t)
    gates = [jnp.where(i1 == j, w1, 0.0) + jnp.where(i2 == j, w2, 0.0) for j in range(PER_GROUP)]
    return g_idx, gates


def _stack_rows(rows, n_rows, tm):
    ri = lax.broadcasted_iota(jnp.int32, (n_rows, tm), 0)
    out = jnp.zeros((n_rows, tm), F32)
    for i, r in enumerate(rows):
        out = jnp.where(ri == i, r, out)
    return out


def _moe_plan(x, g_ref, wr_ref, br_ref):
    tm = MOE_TM
    h = _rmsnorm(x, g_ref[...]).astype(BF16)
    logits = _dot(h, wr_ref[...]) + br_ref[...]
    g_idx, gates = _route(logits.T)

    member = [(g_idx == g).astype(F32) for g in range(N_GROUPS)]
    t_r = lax.broadcasted_iota(jnp.int32, (tm, tm), 0)
    t_c = lax.broadcasted_iota(jnp.int32, (tm, tm), 1)
    upper = jnp.where(t_r <= t_c, 1.0, 0.0).astype(BF16)
    incl = _dot(_stack_rows(member, GATE_ROWS, tm).astype(BF16), upper)
    end_blk = []
    posf = jnp.zeros((1, tm), F32)
    run = jnp.zeros((1, 1), F32)
    for g in range(N_GROUPS):
        n_g = jnp.sum(member[g], axis=-1, keepdims=True)
        posf = posf + member[g] * (run * MOE_BM + incl[g:g + 1, :] - 1.0)
        run = run + jnp.floor((n_g + (MOE_BM - 1)) * (1.0 / MOE_BM))
        end_blk.append(run)

    g_hi = [gt.astype(BF16).astype(F32) for gt in gates]
    g_lo = [gt - hi for gt, hi in zip(gates, g_hi)]
    gmat = _stack_rows(g_hi + g_lo, GATE_ROWS, tm).astype(BF16)
    return h, posf, gmat, end_blk


def _moe_kernel(x_ref, xn_ref, g_ref, wr_ref, br_ref, wg_ref, wu_ref, wd_ref, gf_ref, o_ref,
                ys_ref, h_ref, pos_ref, gmat_ref, cnt_ref, *, final):
    tm = MOE_TM

    def park_plan(x):
        h, posf, gmat, end_blk = _moe_plan(x, g_ref, wr_ref, br_ref)
        h_ref[...] = h
        pos_ref[...] = jnp.broadcast_to(posf, pos_ref.shape)
        gmat_ref[...] = gmat
        for k, e in enumerate(end_blk):
            cnt_ref[k] = e.astype(jnp.int32)[0, 0]

    @pl.when(pl.program_id(0) == 0)
    def _():
        park_plan(x_ref[...])

    posf = pos_ref[0:1, :]
    pos = posf.astype(jnp.int32)
    gmat = gmat_ref[...]
    e0, e1, e2, n_blk = [cnt_ref[k] for k in range(N_GROUPS)]

    r_iota = lax.broadcasted_iota(jnp.int32, (MOE_BM, tm), 0)

    def block(b):
        r0 = pl.multiple_of(b * MOE_BM, MOE_BM)
        grp = (b >= e0).astype(jnp.int32) + (b >= e1).astype(jnp.int32) + (b >= e2).astype(jnp.int32)
        pb = jnp.where(r_iota + r0 == pos, 1.0, 0.0).astype(BF16)
        xb = _dot(pb, h_ref[...]).astype(BF16)
        gs = _dot_nt(pb, gmat)
        hg = _dot(xb, wg_ref[grp])
        hu = _dot(xb, wu_ref[grp])
        gate_cols = jnp.concatenate(
            [jnp.broadcast_to(gs[:, j:j + 1] + gs[:, PER_GROUP + j:PER_GROUP + j + 1], (MOE_BM, D_EXPERT))
             for j in range(PER_GROUP)], axis=-1)
        a = (_silu(hg) * hu * gate_cols).astype(BF16)
        ys_ref[pl.ds(r0, MOE_BM), :] = _dot(a, wd_ref[grp]).astype(BF16)

    n_iter = lax.shift_right_logical(n_blk + (MOE_UNROLL - 1), MOE_UNROLL.bit_length() - 1)

    def blocks(i, carry):
        for u in range(MOE_UNROLL):
            block(i * MOE_UNROLL + u)
        return carry

    lax.fori_loop(0, n_iter, blocks, 0)

    def clear(b, carry):
        r0 = pl.multiple_of(b * MOE_BM, MOE_BM)
        ys_ref[pl.ds(r0, MOE_BM), :] = jnp.zeros((MOE_BM, D_MODEL), BF16)
        return carry

    def unsort_and_plan(width):
        lax.fori_loop(n_iter * MOE_UNROLL, width // MOE_BM, clear, 0)
        pos_col = jnp.broadcast_to(posf, (LANES, tm)).T
        lane = lax.broadcasted_iota(jnp.int32, (tm, LANES), 1).astype(F32)
        pt = jnp.concatenate(
            [jnp.where(pos_col == lane + float(LANES * k), 1.0, 0.0).astype(BF16) for k in range(width // LANES)],
            axis=-1)
        y = x_ref[...] + _dot(pt, ys_ref[0:width, :])
        if final:
            y = _rmsnorm(y, gf_ref[...])
        o_ref[...] = y
        park_plan(xn_ref[...])

    fits = n_iter * (MOE_UNROLL * MOE_BM) <= MOE_R_COMMON

    @pl.when(fits)
    def _():
        unsort_and_plan(MOE_R_COMMON)

    @pl.when(jnp.logical_not(fits))
    def _():
        unsort_and_plan(MOE_R)


def _moe(x2, layer, wts, experts, g_final, *, final):
    n_tiles = x2.shape[0] // MOE_TM
    assert n_tiles * MOE_TM == x2.shape[0] and MOE_R % (MOE_BM * MOE_UNROLL) == 0
    assert MOE_TM + N_GROUPS * (MOE_BM - 1) <= MOE_R
    x_spec = pl.BlockSpec((MOE_TM, D_MODEL), lambda i: (i, 0))
    next_spec = pl.BlockSpec((MOE_TM, D_MODEL), lambda i: (jnp.minimum(i + 1, n_tiles - 1), 0))
    names = ("g_ffn", "w_r", "b_r")
    return pl.pallas_call(
        functools.partial(_moe_kernel, final=final),
        grid=(n_tiles,),
        in_specs=([x_spec, next_spec] + [_layer_spec(wts[n], layer, pipeline_mode=pl.Buffered(1)) for n in names]
                  + [pl.BlockSpec(w.shape, lambda i: (0, 0, 0), pipeline_mode=pl.Buffered(1)) for w in experts]
                  + [pl.BlockSpec(g_final.shape, lambda i: (0, 0))]),
        out_specs=x_spec,
        out_shape=jax.ShapeDtypeStruct(x2.shape, F32),
        scratch_shapes=[pltpu.VMEM((MOE_R, D_MODEL), BF16),
                        pltpu.VMEM((MOE_TM, D_MODEL), BF16),
                        pltpu.VMEM((8, MOE_TM), F32),
                        pltpu.VMEM((GATE_ROWS, MOE_TM), BF16),
                        pltpu.SMEM((N_GROUPS,), jnp.int32)],
        compiler_params=pltpu.CompilerParams(dimension_semantics=("arbitrary",), vmem_limit_bytes=VMEM_LIMIT_BYTES),
        name="moe",
    )(x2, x2, *[wts[n] for n in names], *experts, g_final)


def kernel(x_prompt, x_sample, cache_mem_k, cache_mem_v, state_pool, mem_prompt, g_mix, w_in, g_v, w_s, b_s, w_pool, s_pool, w_out, g_mem, g_xattn, w_xq, w_xk, w_xv, w_xo, g_ffn, w_group, b_group, w_router, b_router, w_gate, w_up, w_down, g_final):
    bp, seq, _ = x_prompt.shape
    bs, dseq, _ = x_sample.shape

    mem_k, mem_v, mem_kb, mem_vb, w_in_b, w_out_b, w_xq_b, w_xo_b, w_fold = _memkv(
        mem_prompt.reshape(bp * N_MEM, D_MODEL), g_mem.reshape(1, D_MODEL), w_xk, w_xv, w_in, w_out, w_xq, w_xo,
        w_pool, s_pool)
    mem_kb = mem_kb.reshape(DEPTH * bp, N_MEM, D_MODEL)
    mem_vb = mem_vb.reshape(DEPTH * bp, N_MEM, D_MODEL)

    pad_r = ROUTER_LANES - N_GROUPS - N_EXPERTS
    bias_full = jnp.repeat(jnp.swapaxes(b_s, 1, 2), HEAD_DIM_A, axis=2)
    wts = {
        "g_mix": g_mix.reshape(DEPTH, 1, D_MODEL), "w_in": w_in_b, "g_v": g_v.reshape(DEPTH, 1, WIDTH_A),
        "w_s": w_s.astype(BF16), "bias_p": bias_full,
        "coef_s": jnp.repeat(jnp.transpose(w_s[:, :, :dseq, :dseq], (0, 3, 2, 1)), HEAD_DIM_A, axis=3),
        "bias_s": bias_full[:, :dseq, :],
        "w_out": w_out_b, "w_fold": w_fold,
        "g_xattn": g_xattn.reshape(DEPTH, 1, D_MODEL), "w_xq": w_xq_b, "w_xo": w_xo_b,
        "g_ffn": g_ffn.reshape(DEPTH, 1, D_MODEL),
        "w_r": jnp.concatenate([w_group, w_router, jnp.zeros((DEPTH, D_MODEL, pad_r), F32)], axis=-1).astype(BF16),
        "b_r": jnp.concatenate([b_group, b_router, jnp.zeros((DEPTH, pad_r), F32)], axis=-1).reshape(DEPTH, 1, ROUTER_LANES),
        "w_gate": w_gate, "w_up": w_up, "w_down": w_down,
    }
    g_final2 = g_final.reshape(1, D_MODEL)

    yp, ys = x_prompt, x_sample
    cv_p, cv_s, pl_p, pl_s = [], [], [], []
    for l in range(DEPTH):
        final = l == DEPTH - 1

        yp, cvp, plp = _mix(yp, None, l, wts, s_t=1, l_t=MIX_TM, pos0=0)
        ys, cvs, pls = _mix(ys, state_pool, l, wts, s_t=SAMPLE_MIX_SEQS, l_t=dseq, pos0=PAST_LEN)
        yp, ys, *experts = _xattn(yp, ys, mem_kb, mem_vb, cache_mem_k, cache_mem_v, l, wts, l_p=PROMPT_TM)
        yp = _moe(yp.reshape(bp * seq, D_MODEL), l, wts, experts, g_final2, final=final).reshape(bp, seq, D_MODEL)
        ys = _moe(ys.reshape(bs * dseq, D_MODEL), l, wts, experts, g_final2, final=final).reshape(bs, dseq, D_MODEL)

        cv_p.append(cvp.reshape(bp, CHUNK, HEADS_A, HEAD_DIM_A))
        cv_s.append(cvs.reshape(bs, dseq, HEADS_A, HEAD_DIM_A))
        pl_p.append(plp)
        pl_s.append(pls)

    return (yp, ys, mem_k, mem_v, jnp.stack(cv_p), jnp.stack(cv_s), jnp.stack(pl_p), jnp.stack(pl_s))
```
